```python
import math
import jax
import jax.numpy as jnp
from jax import lax
import numpy as np

D_MODEL = 1024
BATCH = 4
SEQ = 8192
DEPTH = 2
DEC_BATCH = 32
DEC_SEQ = 1
PAST_LEN = 16384
PAGE_SIZE = 128

HEAD_DIM = 64
CONV_CH = 256
CONV_WIDTH = 31
NSA_HEADS = 8
NSA_KV_HEADS = 2
NSA_GROUP = NSA_HEADS // NSA_KV_HEADS
FOX_HEADS = 4
CMP_LEN = 32
CMP_STRIDE = 16
SEL_LEN = 64
SEL_TOPK = 16
WINDOW = 512
D_FF = 2816
Q_BLOCK = 128
RMS_EPS = 1e-6
LN_EPS = 1e-5
FORCE_SCORE = 1e9

NSA_Q = NSA_HEADS * HEAD_DIM
NSA_KV = NSA_KV_HEADS * HEAD_DIM
FOX_W = FOX_HEADS * HEAD_DIM
MIX_WIDTH = CONV_CH + NSA_Q + FOX_W
SPLIT_POINTS = (CONV_CH,
                2 * CONV_CH,
                2 * CONV_CH + NSA_Q,
                2 * CONV_CH + NSA_Q + 6 * NSA_KV,
                2 * CONV_CH + NSA_Q + 6 * NSA_KV + 3 * NSA_HEADS,
                2 * CONV_CH + NSA_Q + 6 * NSA_KV + 3 * NSA_HEADS + 3 * FOX_W)
IN_COLS = 2 * CONV_CH + NSA_Q + 6 * NSA_KV + 3 * NSA_HEADS + 3 * FOX_W + FOX_HEADS

kernel_name = 'hymba_conv_nsa_fox_decoder_step'


def rmsnorm(x, g):
    xf = x.astype(jnp.float32)
    y = xf * lax.rsqrt(jnp.mean(xf * xf, axis=-1, keepdims=True) + RMS_EPS)
    return (y * g).astype(x.dtype)


def layernorm(x, g, b):
    xf = x.astype(jnp.float32)
    mu = jnp.mean(xf, axis=-1, keepdims=True)
    var = jnp.mean(jnp.square(xf - mu), axis=-1, keepdims=True)
    return ((xf - mu) * lax.rsqrt(var + LN_EPS) * g + b).astype(x.dtype)


def swiglu(x, w_gate, w_up, w_down):
    return (jax.nn.silu(x @ w_gate) * (x @ w_up)) @ w_down


def ffn_half_step(x, g_pre, g_post, w_gate, w_up, w_down):
    return x + 0.5 * rmsnorm(swiglu(rmsnorm(x, g_pre), w_gate, w_up, w_down), g_post)


def masked_softmax(s, mask):
    s = jnp.where(mask, s, -jnp.inf)
    m = jnp.max(s, axis=-1, keepdims=True)
    m = jnp.where(jnp.isfinite(m), m, 0.0)
    e = jnp.exp(s - m)
    d = jnp.sum(e, axis=-1, keepdims=True)
    return e / jnp.where(d > 0, d, 1.0)


def alibi_slopes():
    sl = 2.0 ** (-8.0 * np.arange(1, NSA_HEADS + 1) / NSA_HEADS)
    return jnp.asarray(sl.astype(np.float32)).reshape(NSA_KV_HEADS, NSA_GROUP)


def project(h, lp):
    B, T, _ = h.shape
    z = h @ lp['w_in']
    a, g, qn, kvn, gn, qkvf, ff = jnp.split(z, SPLIT_POINTS, axis=-1)
    u = a * jax.nn.sigmoid(g)
    qn = qn.reshape(B, T, NSA_KV_HEADS, NSA_GROUP, HEAD_DIM) * (HEAD_DIM ** -0.5)
    kvn = kvn.reshape(B, T, 6, NSA_KV_HEADS, HEAD_DIM)
    gates = jax.nn.sigmoid(gn + lp['gate_b']).reshape(B, T, NSA_KV_HEADS, NSA_GROUP, 3)
    qkvf = qkvf.reshape(B, T, 3, FOX_HEADS, HEAD_DIM)
    qf = qkvf[:, :, 0] * (HEAD_DIM ** -0.5)
    kvf = qkvf[:, :, 1:]
    logf = jax.nn.log_sigmoid((ff + lp['forget_b']).astype(jnp.float32))
    return u, qn, kvn, gates, qf, kvf, logf


def conv_module(u_hist, lp):
    w = lp['conv_w'][:, None, :].astype(u_hist.dtype)
    y = lax.conv_general_dilated(u_hist, w, (1,), 'VALID',
                                 dimension_numbers=('NWC', 'WIO', 'NWC'),
                                 feature_group_count=CONV_CH) + lp['conv_b']
    return jax.nn.silu(layernorm(y, lp['conv_ln_g'], lp['conv_ln_b']))


def compress(k, pe, w):
    B, L, G, dk = k.shape
    n_chunk = L // CMP_STRIDE
    ch = k[:, :n_chunk * CMP_STRIDE].reshape(B, n_chunk, CMP_STRIDE, G, dk)
    w3 = w.reshape(CMP_LEN, dk, dk)
    first = jnp.einsum('bnlgd,lde->bnge', ch + pe[:CMP_STRIDE, None, :], w3[:CMP_STRIDE])
    second = jnp.einsum('bnlgd,lde->bnge', ch + pe[CMP_STRIDE:, None, :], w3[CMP_STRIDE:])
    return first[:, :-1] + second[:, 1:]


def nsa_attend(q, qpos, kc, vc, k_sel, v_sel, kw, vw, kw_pos, gates, slopes):
    f32 = jnp.float32
    B, Tq, G, R, dk = q.shape
    Nc = kc.shape[1]
    Ns = k_sel.shape[1]
    c_end = jnp.arange(Nc, dtype=jnp.int32) * CMP_STRIDE + CMP_LEN - 1
    dist_c = qpos[:, None] - c_end[None, :]
    s_c = jnp.einsum('btgrd,bngd->btgrn', q, kc).astype(f32)
    s_c = s_c - slopes[:, :, None] * dist_c[:, None, None, :].astype(f32)
    p_c = masked_softmax(s_c, (dist_c >= 0)[:, None, None, :])
    o_c = jnp.einsum('btgrn,bngd->btgrd', p_c.astype(vc.dtype), vc)
    ratio = SEL_LEN // CMP_STRIDE
    imp = jnp.sum(p_c, axis=3)
    imp = jnp.pad(imp, ((0, 0), (0, 0), (0, 0), (1, ratio * Ns - Nc)))
    s1 = imp[..., 1:] + imp[..., :-1]
    imp_sel = jnp.sum(s1.reshape(B, Tq, G, Ns, ratio), axis=-1)
    blk = jnp.arange(Ns, dtype=jnp.int32)
    cur = qpos // SEL_LEN
    forced = (blk[None, :] == 0) | (blk[None, :] == cur[:, None]) | (blk[None, :] == cur[:, None] - 1)
    score = jnp.where(forced[None, :, None, :], FORCE_SCORE, imp_sel)
    score = jnp.where((blk[None, :] > cur[:, None])[None, :, None, :], -jnp.inf, score)
    n_sel = min(SEL_TOPK, Ns)
    _, sel = lax.top_k(score, n_sel)
    gather = jax.vmap(jax.vmap(lambda blocks, idx: blocks[idx]))
    sel_t = jnp.transpose(sel, (0, 2, 1, 3))
    ks = gather(jnp.moveaxis(k_sel, 3, 1), sel_t)
    vs = gather(jnp.moveaxis(v_sel, 3, 1), sel_t).reshape(B, G, Tq, n_sel * SEL_LEN, dk)
    s_s = jnp.einsum('btgrd,bgtnld->btgrnl', q, ks).astype(f32).reshape(B, Tq, G, R, n_sel * SEL_LEN)
    kpos = sel[..., None] * SEL_LEN + jnp.arange(SEL_LEN, dtype=jnp.int32)
    dist_s = (qpos[None, :, None, None, None] - kpos).reshape(B, Tq, G, 1, n_sel * SEL_LEN)
    s_s = s_s - slopes[:, :, None] * dist_s.astype(f32)
    p_s = masked_softmax(s_s, dist_s >= 0)
    o_s = jnp.einsum('btgrk,bgtkd->btgrd', p_s.astype(vs.dtype), vs)
    dist_w = qpos[:, None] - kw_pos[None, :]
    mask_w = (dist_w >= 0) & (dist_w <= WINDOW) & (kw_pos[None, :] >= 0)
    s_w = jnp.einsum('btgrd,bkgd->btgrk', q, kw).astype(f32)
    s_w = s_w - slopes[:, :, None] * dist_w[:, None, None, :].astype(f32)
    p_w = masked_softmax(s_w, mask_w[:, None, None, :])
    o_w = jnp.einsum('btgrk,bkgd->btgrd', p_w.astype(vw.dtype), vw)
    return gates[..., 0:1] * o_c + gates[..., 1:2] * o_s + gates[..., 2:3] * o_w


def nsa_prompt(q, gates, kc, vc, k_sel, v_sel, k_win, v_win, slopes):
    B, T = q.shape[:2]
    pad = ((0, 0), (WINDOW, 0), (0, 0), (0, 0))
    kw_pad = jnp.pad(k_win, pad)
    vw_pad = jnp.pad(v_win, pad)

    def one_block(b):
        t0 = b * Q_BLOCK
        qb = lax.dynamic_slice_in_dim(q, t0, Q_BLOCK, axis=1)
        gb = lax.dynamic_slice_in_dim(gates, t0, Q_BLOCK, axis=1)
        kwb = lax.dynamic_slice_in_dim(kw_pad, t0, WINDOW + Q_BLOCK, axis=1)
        vwb = lax.dynamic_slice_in_dim(vw_pad, t0, WINDOW + Q_BLOCK, axis=1)
        qpos = t0 + jnp.arange(Q_BLOCK, dtype=jnp.int32)
        kw_pos = t0 - WINDOW + jnp.arange(WINDOW + Q_BLOCK, dtype=jnp.int32)
        return nsa_attend(qb, qpos, kc, vc, k_sel, v_sel, kwb, vwb, kw_pos, gb, slopes)

    out = lax.map(one_block, jnp.arange(T // Q_BLOCK, dtype=jnp.int32))
    return jnp.moveaxis(out, 0, 1).reshape(q.shape)


def fox_attend(q, qpos, k, v, kpos, cq, ck):
    s = jnp.einsum('bthd,bshd->bhts', q, k).astype(jnp.float32)
    s = s + (jnp.transpose(cq, (0, 2, 1))[..., :, None] - jnp.transpose(ck, (0, 2, 1))[..., None, :])
    p = masked_softmax(s, (kpos[None, :] <= qpos[:, None])[None, None])
    return jnp.einsum('bhts,bshd->bthd', p.astype(v.dtype), v)


def fox_prompt(q, k, v, c):
    B, T = q.shape[:2]
    kpos = jnp.arange(T, dtype=jnp.int32)

    def one_block(b):
        t0 = b * Q_BLOCK
        qb = lax.dynamic_slice_in_dim(q, t0, Q_BLOCK, axis=1)
        cb = lax.dynamic_slice_in_dim(c, t0, Q_BLOCK, axis=1)
        return fox_attend(qb, t0 + jnp.arange(Q_BLOCK, dtype=jnp.int32), k, v, kpos, cb, c)

    out = lax.map(one_block, jnp.arange(T // Q_BLOCK, dtype=jnp.int32))
    return jnp.moveaxis(out, 0, 1).reshape(q.shape)


def merge_heads(y_conv, o_nsa, o_fox, w_out):
    B, T = y_conv.shape[:2]
    cat = jnp.concatenate([y_conv, o_nsa.reshape(B, T, NSA_Q).astype(y_conv.dtype),
                           o_fox.reshape(B, T, FOX_W).astype(y_conv.dtype)], axis=-1)
    return cat @ w_out


def mixer_prompt(h, lp, slopes):
    B, T, _ = h.shape
    u, qn, kvn, gates, qf, kvf, logf = project(h, lp)
    u_pad = jnp.pad(u, ((0, 0), (CONV_WIDTH - 1, 0), (0, 0)))
    y_conv = conv_module(u_pad, lp)
    conv_state = u_pad[:, u_pad.shape[1] - (CONV_WIDTH - 1):]
    kc = compress(kvn[:, :, 0], lp['cmp_pe'][0], lp['cmp_w'][0])
    vc = compress(kvn[:, :, 1], lp['cmp_pe'][1], lp['cmp_w'][1])
    k_sel = kvn[:, :, 2].reshape(B, T // SEL_LEN, SEL_LEN, NSA_KV_HEADS, HEAD_DIM)
    v_sel = kvn[:, :, 3].reshape(B, T // SEL_LEN, SEL_LEN, NSA_KV_HEADS, HEAD_DIM)
    o_nsa = nsa_prompt(qn, gates, kc, vc, k_sel, v_sel, kvn[:, :, 4], kvn[:, :, 5], slopes)
    win_rows = jnp.pad(kvn[:, :, 4:6], ((0, 0), (max(WINDOW - T, 0), 0), (0, 0), (0, 0), (0, 0)))
    win_state = win_rows[:, win_rows.shape[1] - WINDOW:]
    c = lax.cumsum(logf, axis=1)
    o_fox = fox_prompt(qf, kvf[:, :, 0], kvf[:, :, 1], c)
    out = merge_heads(y_conv, o_nsa, o_fox, lp['w_out'])
    return out, kvn[:, :, :4], win_state, kvf, logf, conv_state


def mixer_sample(h, nsa_pages, win_buf, fox_pages, logf_pages, conv_buf, page_table, lp, slopes):
    B, T, _ = h.shape
    past = page_table.shape[1] * PAGE_SIZE
    L = past + T
    u, qn, kvn, gates, qf, kvf, logf = project(h, lp)
    u_hist = jnp.concatenate([conv_buf.astype(u.dtype), u], axis=1)
    y_conv = conv_module(u_hist, lp)
    conv_state = u_hist[:, u_hist.shape[1] - (CONV_WIDTH - 1):]
    past_nsa = nsa_pages[page_table].reshape(B, past, 4, NSA_KV_HEADS, HEAD_DIM)
    full = jnp.concatenate([past_nsa.astype(kvn.dtype), kvn[:, :, :4]], axis=1)
    kc = compress(full[:, :, 0], lp['cmp_pe'][0], lp['cmp_w'][0])
    vc = compress(full[:, :, 1], lp['cmp_pe'][1], lp['cmp_w'][1])
    ns = -(-L // SEL_LEN)
    sel_rows = jnp.pad(full[:, :, 2:4], ((0, 0), (0, ns * SEL_LEN - L), (0, 0), (0, 0), (0, 0)))
    sel_rows = sel_rows.reshape(B, ns, SEL_LEN, 2, NSA_KV_HEADS, HEAD_DIM)
    wb = win_buf.shape[1]
    win = jnp.concatenate([win_buf.astype(kvn.dtype), kvn[:, :, 4:6]], axis=1)
    kw_pos = past - wb + jnp.arange(wb + T, dtype=jnp.int32)
    qpos = past + jnp.arange(T, dtype=jnp.int32)
    o_nsa = nsa_attend(qn, qpos, kc, vc, sel_rows[:, :, :, 0], sel_rows[:, :, :, 1],
                       win[:, :, 0], win[:, :, 1], kw_pos, gates, slopes)
    win_state = win[:, win.shape[1] - wb:]
    past_f = fox_pages[page_table].reshape(B, past, 2, FOX_HEADS, HEAD_DIM)
    kf = jnp.concatenate([past_f.astype(kvf.dtype), kvf], axis=1)
    lf = jnp.concatenate([logf_pages[page_table].reshape(B, past, FOX_HEADS).astype(jnp.float32), logf], axis=1)
    c = lax.cumsum(lf, axis=1)
    o_fox = fox_attend(qf, qpos, kf[:, :, 0], kf[:, :, 1], jnp.arange(L, dtype=jnp.int32), c[:, past:], c)
    out = merge_heads(y_conv, o_nsa, o_fox, lp['w_out'])
    return out, kvn[:, :, :4], win_state, kvf, logf, conv_state


def setup_inputs(seed: int = 0) -> dict:
    key = jax.random.key(seed)
    ks = jax.random.split(key, 24)
    f32 = jnp.float32
    n_pages = PAST_LEN // PAGE_SIZE
    n_phys = (5 * DEC_BATCH * n_pages) // 4
    wb = min(WINDOW, PAST_LEN)

    def nrm(k, shape, s=1.0):
        return s * jax.random.normal(k, shape, f32)

    return {
        'x_prompt': nrm(ks[0], (BATCH, SEQ, D_MODEL)),
        'x_sample': nrm(ks[1], (DEC_BATCH, DEC_SEQ, D_MODEL)),
        'cache_nsa_kv': nrm(ks[2], (DEPTH, n_phys, PAGE_SIZE, 4, NSA_KV_HEADS, HEAD_DIM)),
        'state_nsa_win': nrm(ks[3], (DEPTH, DEC_BATCH, wb, 2, NSA_KV_HEADS, HEAD_DIM)),
        'cache_fox_kv': nrm(ks[4], (DEPTH, n_phys, PAGE_SIZE, 2, FOX_HEADS, HEAD_DIM)),
        'cache_fox_logf': jax.nn.log_sigmoid(3.0 + nrm(ks[5], (DEPTH, n_phys, PAGE_SIZE, FOX_HEADS))),
        'state_conv': nrm(ks[6], (DEPTH, DEC_BATCH, CONV_WIDTH - 1, CONV_CH), 0.5),
        'page_table': jax.random.permutation(ks[7], n_phys)[:DEC_BATCH * n_pages].reshape(DEC_BATCH, n_pages).astype(jnp.int32),
        'norm_g': 1.0 + nrm(ks[8], (DEPTH, 6, D_MODEL), 0.02),
        'ffn_w_gate': nrm(ks[9], (DEPTH, 2, D_MODEL, D_FF), D_MODEL ** -0.5),
        'ffn_w_up': nrm(ks[10], (DEPTH, 2, D_MODEL, D_FF), D_MODEL ** -0.5),
        'ffn_w_down': nrm(ks[11], (DEPTH, 2, D_FF, D_MODEL), D_FF ** -0.5),
        'w_in': nrm(ks[12], (DEPTH, D_MODEL, IN_COLS), D_MODEL ** -0.5),
        'w_out': nrm(ks[13], (DEPTH, MIX_WIDTH, D_MODEL), MIX_WIDTH ** -0.5),
        'conv_w': nrm(ks[14], (DEPTH, CONV_WIDTH, CONV_CH), CONV_WIDTH ** -0.5),
        'conv_b': nrm(ks[15], (DEPTH, CONV_CH), 0.01),
        'conv_ln_g': 1.0 + nrm(ks[16], (DEPTH, CONV_CH), 0.02),
        'conv_ln_b': nrm(ks[17], (DEPTH, CONV_CH), 0.01),
        'cmp_pe': nrm(ks[18], (DEPTH, 2, CMP_LEN, HEAD_DIM), 0.1),
        'cmp_w': nrm(ks[19], (DEPTH, 2, CMP_LEN * HEAD_DIM, HEAD_DIM), (CMP_LEN * HEAD_DIM) ** -0.5),
        'nsa_gate_b': nrm(ks[20], (DEPTH, 3 * NSA_HEADS), 0.01),
        'fox_forget_b': 3.0 + nrm(ks[21], (DEPTH, FOX_HEADS), 0.5),
    }


def reference(x_prompt, x_sample, cache_nsa_kv, state_nsa_win, cache_fox_kv, cache_fox_logf, state_conv,
              page_table, norm_g, ffn_w_gate, ffn_w_up, ffn_w_down, w_in, w_out, conv_w, conv_b,
              conv_ln_g, conv_ln_b, cmp_pe, cmp_w, nsa_gate_b, fox_forget_b):
    slopes = alibi_slopes()
    xp, xs = x_prompt, x_sample
    p_nsa, p_win, p_fkv, p_flf, p_conv = [], [], [], [], []
    s_nsa, s_win, s_fkv, s_flf, s_conv = [], [], [], [], []
    for l in range(DEPTH):
        ng = norm_g[l]
        lp = {'w_in': w_in[l], 'w_out': w_out[l], 'conv_w': conv_w[l], 'conv_b': conv_b[l],
              'conv_ln_g': conv_ln_g[l], 'conv_ln_b': conv_ln_b[l], 'cmp_pe': cmp_pe[l], 'cmp_w': cmp_w[l],
              'gate_b': nsa_gate_b[l], 'forget_b': fox_forget_b[l]}
        xp = ffn_half_step(xp, ng[0], ng[1], ffn_w_gate[l, 0], ffn_w_up[l, 0], ffn_w_down[l, 0])
        xs = ffn_half_step(xs, ng[0], ng[1], ffn_w_gate[l, 0], ffn_w_up[l, 0], ffn_w_down[l, 0])
        mp, a0, a1, a2, a3, a4 = mixer_prompt(rmsnorm(xp, ng[2]), lp, slopes)
        ms, b0, b1, b2, b3, b4 = mixer_sample(rmsnorm(xs, ng[2]), cache_nsa_kv[l], state_nsa_win[l],
                                              cache_fox_kv[l], cache_fox_logf[l], state_conv[l],
                                              page_table, lp, slopes)
        xp = xp + rmsnorm(mp, ng[3])
        xs = xs + rmsnorm(ms, ng[3])
        xp = ffn_half_step(xp, ng[4], ng[5], ffn_w_gate[l, 1], ffn_w_up[l, 1], ffn_w_down[l, 1])
        xs = ffn_half_step(xs, ng[4], ng[5], ffn_w_gate[l, 1], ffn_w_up[l, 1], ffn_w_down[l, 1])
        p_nsa.append(a0); p_win.append(a1); p_fkv.append(a2); p_flf.append(a3); p_conv.append(a4)
        s_nsa.append(b0); s_win.append(b1); s_fkv.append(b2); s_flf.append(b3); s_conv.append(b4)
    return (xp, xs,
            jnp.stack(p_nsa), jnp.stack(p_win), jnp.stack(p_fkv), jnp.stack(p_flf), jnp.stack(p_conv),
            jnp.stack(s_nsa), jnp.stack(s_win), jnp.stack(s_fkv), jnp.stack(s_flf), jnp.stack(s_conv))
```

```python
import functools

import jax
import jax.numpy as jnp
import numpy as np
from jax import lax
from jax.experimental import pallas as pl
from jax.experimental.pallas import tpu as pltpu

F32 = jnp.float32
BF16 = jnp.bfloat16

D_MODEL = 1024
D_FF = 2816
HEAD_DIM = 64
CONV_CH = 256
CONV_WIDTH = 31
NSA_HEADS = 8
NSA_KV_HEADS = 2
NSA_GROUP = NSA_HEADS // NSA_KV_HEADS
FOX_HEADS = 4
CMP_LEN = 32
CMP_STRIDE = 16
SEL_LEN = 64
SEL_TOPK = 16
WINDOW = 512
PAGE_SIZE = 128
RMS_EPS = 1e-6
LN_EPS = 1e-5
FORCE_SCORE = 1e9

NSA_Q = NSA_HEADS * HEAD_DIM
NSA_KV = NSA_KV_HEADS * HEAD_DIM
FOX_W = FOX_HEADS * HEAD_DIM
N_GATES = 3 * NSA_HEADS

LANES = 128
V7X_VMEM_BYTES = 64 * 1024 * 1024
VMEM_LIMIT = (V7X_VMEM_BYTES * 7) // 8

MASKED = -1e30
FF_CHUNK = 256
CONV_HALO = 32

C_GLU = 0
C_QN = C_GLU + 2 * CONV_CH
C_NKV = C_QN + NSA_HEADS * LANES
C_WIN = C_NKV + 4 * NSA_KV
C_QF = C_WIN + 2 * NSA_KV
C_FKV = C_QF + FOX_HEADS * LANES
C_SM = C_FKV + 2 * FOX_W
PROJ_COLS = C_SM + LANES
SM_LOGF = N_GATES

NT_DIMS = (((1,), (1,)), ((), ()))


def _cparams(n_grid):
    return pltpu.CompilerParams(dimension_semantics=("arbitrary",) * n_grid,
                                vmem_limit_bytes=VMEM_LIMIT)


def _const_spec(shape, index):
    return pl.BlockSpec(shape, index, pipeline_mode=pl.Buffered(1))


def _rms(x, g):
    return x * lax.rsqrt(jnp.mean(x * x, axis=-1, keepdims=True) + RMS_EPS) * g


def _split3(x):
    hi = x.astype(BF16)
    r1 = x - hi.astype(F32)
    mid = r1.astype(BF16)
    lo = (r1 - mid.astype(F32)).astype(BF16)
    return hi, mid, lo


def _dot(a, b):
    return jnp.dot(a, b, preferred_element_type=F32)


def _dot_nt(a, b):
    return lax.dot_general(a, b, NT_DIMS, preferred_element_type=F32)


def _ffn_math(x, gpre, gpost, wg_ref, wu_ref, wd_ref):
    xn = _rms(x, gpre).astype(BF16)
    acc = jnp.zeros(x.shape, F32)
    for c in range(D_FF // FF_CHUNK):
        sl = slice(c * FF_CHUNK, (c + 1) * FF_CHUNK)
        g = _dot(xn, wg_ref[:, sl])
        u = _dot(xn, wu_ref[:, sl])
        h = (g * jax.nn.sigmoid(g) * u).astype(BF16)
        acc = acc + _dot(h, wd_ref[sl, :])
    return x + 0.5 * _rms(acc, gpost)


def _ffn_body(x_ref, gpre_ref, gpost_ref, wg_ref, wu_ref, wd_ref, o_ref):
    o_ref[...] = _ffn_math(x_ref[...], gpre_ref[...], gpost_ref[...], wg_ref, wu_ref, wd_ref)


def _ffn(x, norm_g, wg, wu, wd, l, k, tm):
    B, T, D = x.shape
    gi = 4 * k
    return pl.pallas_call(
        _ffn_body,
        out_shape=jax.ShapeDtypeStruct(x.shape, F32),
        grid=(B, T // tm),
        in_specs=[
            pl.BlockSpec((None, tm, D), lambda b, i: (b, i, 0)),
            _const_spec((None, None, 1, D), lambda b, i: (l, gi, 0, 0)),
            _const_spec((None, None, 1, D), lambda b, i: (l, gi + 1, 0, 0)),
            _const_spec((None, None, D, D_FF), lambda b, i: (l, k, 0, 0)),
            _const_spec((None, None, D, D_FF), lambda b, i: (l, k, 0, 0)),
            _const_spec((None, None, D_FF, D), lambda b, i: (l, k, 0, 0)),
        ],
        out_specs=pl.BlockSpec((None, tm, D), lambda b, i: (b, i, 0)),
        compiler_params=_cparams(2),
        name="ffn_half_step",
    )(x, norm_g, norm_g, wg, wu, wd)


def _proj_body(x_ref, g_ref, w_ref, b_ref, u_ref, qn_ref, nkv_ref, nkvb_ref, win_ref, winb_ref,
               qf_ref, fkv_ref, fkvb_ref, sm_ref):
    xn = _rms(x_ref[...], g_ref[...]).astype(BF16)

    def mm(lo, hi):
        return _dot(xn, w_ref[:, lo:hi])

    z = mm(C_GLU, C_QN)
    u_ref[...] = z[:, :CONV_CH] * jax.nn.sigmoid(z[:, CONV_CH:])
    qn_ref[...] = (mm(C_QN, C_NKV) * (HEAD_DIM ** -0.5)).astype(BF16)
    z = mm(C_NKV, C_WIN)
    nkv_ref[...] = z
    nkvb_ref[...] = z.astype(BF16)
    z = mm(C_WIN, C_QF)
    win_ref[...] = z
    winb_ref[...] = z.astype(BF16)
    qf_ref[...] = (mm(C_QF, C_FKV) * (HEAD_DIM ** -0.5)).astype(BF16)
    z = mm(C_FKV, C_SM)
    fkv_ref[...] = z
    fkvb_ref[...] = z.astype(BF16)
    z = mm(C_SM, PROJ_COLS) + b_ref[...]
    lane = lax.broadcasted_iota(jnp.int32, z.shape, 1)
    log_sig = jnp.minimum(z, 0.0) - jnp.log1p(jnp.exp(-jnp.abs(z)))
    sm_ref[...] = jnp.where(lane < N_GATES, jax.nn.sigmoid(z), log_sig)


def _proj(x, norm_g, w_proj, b_small, l, tm):
    B, T, D = x.shape
    widths = [(CONV_CH, F32), (NSA_HEADS * LANES, BF16), (4 * NSA_KV, F32), (4 * NSA_KV, BF16),
              (2 * NSA_KV, F32), (2 * NSA_KV, BF16), (FOX_HEADS * LANES, BF16), (2 * FOX_W, F32),
              (2 * FOX_W, BF16), (LANES, F32)]
    return pl.pallas_call(
        _proj_body,
        out_shape=[jax.ShapeDtypeStruct((B, T, w), dt) for w, dt in widths],
        grid=(B, T // tm),
        in_specs=[
            pl.BlockSpec((None, tm, D), lambda b, i: (b, i, 0)),
            _const_spec((None, None, 1, D), lambda b, i: (l, 2, 0, 0)),
            _const_spec((None, D, PROJ_COLS), lambda b, i: (l, 0, 0)),
            _const_spec((None, 1, LANES), lambda b, i: (l, 0, 0)),
        ],
        out_specs=[pl.BlockSpec((None, tm, w), lambda b, i: (b, i, 0)) for w, _ in widths],
        compiler_params=_cparams(2),
        name="input_projection",
    )(x, norm_g, w_proj, b_small)


def _conv_body(prev_ref, cur_ref, w_ref, b_ref, lng_ref, lnb_ref, o_ref, hist_ref, *, tc):
    i = pl.program_id(1)
    prev = prev_ref[...]
    hist_ref[0:CONV_HALO, :] = jnp.where(i == 0, 0.0, prev)
    hist_ref[CONV_HALO:CONV_HALO + tc, :] = cur_ref[...]
    off = CONV_HALO - (CONV_WIDTH - 1)
    acc = jnp.zeros((tc, CONV_CH), F32)
    for k in range(CONV_WIDTH):
        acc = acc + hist_ref[off + k:off + k + tc, :] * w_ref[k:k + 1, :]
    o_ref[...] = _conv_post(acc, b_ref[...], lng_ref[...], lnb_ref[...]).astype(o_ref.dtype)


def _conv_post(acc, b, ln_g, ln_b):
    y = acc + b
    mu = jnp.mean(y, axis=-1, keepdims=True)
    var = jnp.mean(jnp.square(y - mu), axis=-1, keepdims=True)
    y = (y - mu) * lax.rsqrt(var + LN_EPS) * ln_g + ln_b
    return y * jax.nn.sigmoid(y)


def _conv_prompt(u, conv_w, conv_b, ln_g, ln_b, l, tc):
    B, T, C = u.shape
    halo_blocks = tc // CONV_HALO
    return pl.pallas_call(
        functools.partial(_conv_body, tc=tc),
        out_shape=jax.ShapeDtypeStruct((B, T, C), BF16),
        grid=(B, T // tc),
        in_specs=[
            pl.BlockSpec((None, CONV_HALO, C), lambda b, i: (b, jnp.maximum(i * halo_blocks - 1, 0), 0)),
            pl.BlockSpec((None, tc, C), lambda b, i: (b, i, 0)),
            _const_spec((None, CONV_WIDTH, C), lambda b, i: (l, 0, 0)),
            _const_spec((None, 1, C), lambda b, i: (l, 0, 0)),
            _const_spec((None, 1, C), lambda b, i: (l, 0, 0)),
            _const_spec((None, 1, C), lambda b, i: (l, 0, 0)),
        ],
        out_specs=pl.BlockSpec((None, tc, C), lambda b, i: (b, i, 0)),
        scratch_shapes=[pltpu.VMEM((CONV_HALO + tc, C), F32)],
        compiler_params=_cparams(2),
        name="conv_module_prompt",
    )(u, u, conv_w, conv_b, ln_g, ln_b)


def _compress_halves(load_rows, w_ref, pe_ref, n):
    first = jnp.zeros((n, 2 * NSA_KV), F32)
    second = jnp.zeros((n, 2 * NSA_KV), F32)
    for l in range(CMP_STRIDE):
        xl = load_rows(l)
        a = (xl + pe_ref[l]).astype(BF16)
        b = (xl + pe_ref[CMP_STRIDE + l]).astype(BF16)
        first = first + _dot(a, w_ref[l])
        second = second + _dot(b, w_ref[CMP_STRIDE + l])
    return first, second


def _compress_prompt_body(xk_ref, xv_ref, w_ref, pe_ref, o_ref, *, n):
    def load(l):
        return jnp.concatenate([r[pl.ds(l, n, stride=CMP_STRIDE), :] for r in (xk_ref, xv_ref)], axis=1)

    first, second = _compress_halves(load, w_ref, pe_ref, n)
    o_ref[...] = (first + pltpu.roll(second, n - 1, axis=0)).astype(BF16)


def _compress_prompt(nkv, w_cmp, pe_cmp, l):
    B, T, _ = nkv.shape
    n = T // CMP_STRIDE
    return pl.pallas_call(
        functools.partial(_compress_prompt_body, n=n),
        out_shape=jax.ShapeDtypeStruct((B, n, 2 * NSA_KV), BF16),
        grid=(B,),
        in_specs=[
            pl.BlockSpec((None, T, NSA_KV), lambda b: (b, 0, 0)),
            pl.BlockSpec((None, T, NSA_KV), lambda b: (b, 0, 1)),
            _const_spec((None, CMP_LEN, 2 * NSA_KV, 2 * NSA_KV), lambda b: (l, 0, 0, 0)),
            _const_spec((None, CMP_LEN, 1, 2 * NSA_KV), lambda b: (l, 0, 0, 0)),
        ],
        out_specs=pl.BlockSpec((None, n, 2 * NSA_KV), lambda b: (b, 0, 0)),
        compiler_params=_cparams(1),
        name="nsa_compress_prompt",
    )(nkv, nkv, w_cmp, pe_cmp)


def _cumsum_body(sm_ref, tri_ref, sel_ref, ccol_ref, crow_ref, carry_ref):
    @pl.when(pl.program_id(1) == 0)
    def _():
        carry_ref[...] = jnp.zeros_like(carry_ref)

    tri = tri_ref[...]
    c = carry_ref[...]
    for part in _split3(sm_ref[...]):
        c = c + _dot(tri, part)
    carry_ref[...] = c[c.shape[0] - 1:, :]
    ccol_ref[...] = c
    sel = sel_ref[...]
    crow = jnp.zeros(crow_ref.shape, F32)
    for part in _split3(c):
        crow = crow + _dot_nt(sel, part)
    crow_ref[...] = crow


def _cumsum(sm, tk):
    B, T, _ = sm.shape
    tri = jnp.asarray(np.tril(np.ones((tk, tk), np.float32)), BF16)
    sel = np.zeros((8, LANES), np.float32)
    sel[np.arange(FOX_HEADS), SM_LOGF + np.arange(FOX_HEADS)] = 1.0
    return pl.pallas_call(
        _cumsum_body,
        out_shape=[jax.ShapeDtypeStruct((B, T, LANES), F32),
                   jax.ShapeDtypeStruct((B, T // tk, 8, tk), F32)],
        grid=(B, T // tk),
        in_specs=[
            pl.BlockSpec((None, tk, LANES), lambda b, i: (b, i, 0)),
            _const_spec((tk, tk), lambda b, i: (0, 0)),
            _const_spec((8, LANES), lambda b, i: (0, 0)),
        ],
        out_specs=[pl.BlockSpec((None, tk, LANES), lambda b, i: (b, i, 0)),
                   pl.BlockSpec((None, None, 8, tk), lambda b, i: (b, i, 0, 0))],
        scratch_shapes=[pltpu.VMEM((1, LANES), F32)],
        compiler_params=_cparams(2),
        name="fox_cumsum",
    )(sm, tri, jnp.asarray(sel, BF16))


def _online_update(s, v, m_ref, l_ref, acc_ref):
    m_old = m_ref[...]
    m_new = jnp.maximum(m_old, jnp.max(s, axis=-1, keepdims=True))
    e = jnp.exp(s - m_new)
    alpha = jnp.exp(m_old - m_new)
    l_ref[...] = alpha * l_ref[...] + jnp.sum(e, axis=-1, keepdims=True)
    acc_ref[...] = alpha * acc_ref[...] + _dot(e.astype(BF16), v)
    m_ref[...] = m_new


def _fox_body(q_ref, ccol_ref, crow_ref, kv_ref, o_ref, m_ref, l_ref, acc_ref, *, tq, tk):
    i = pl.program_id(1)
    t0 = i * tq
    n_tiles = (t0 + tq + tk - 1) // tk
    qpos = t0 + lax.broadcasted_iota(jnp.int32, (tq, 1), 0)
    qpos2 = jnp.concatenate([qpos, qpos], axis=0)
    lane = lax.broadcasted_iota(jnp.int32, (tq, LANES), 1)
    for p in range(FOX_HEADS // 2):
        q = jnp.concatenate([q_ref[:, (2 * p + hh) * LANES:(2 * p + hh + 1) * LANES] for hh in range(2)], axis=0)
        cq = [ccol_ref[:, SM_LOGF + 2 * p + hh:SM_LOGF + 2 * p + hh + 1] for hh in range(2)]
        m_ref[...] = jnp.full(m_ref.shape, MASKED, F32)
        l_ref[...] = jnp.zeros(l_ref.shape, F32)
        acc_ref[...] = jnp.zeros(acc_ref.shape, F32)

        def tile(j, carry, p=p, q=q, cq=cq):
            ks = pl.multiple_of(j * tk, tk)
            k = kv_ref[pl.ds(ks, tk), p * LANES:(p + 1) * LANES]
            v = kv_ref[pl.ds(ks, tk), FOX_W + p * LANES:FOX_W + (p + 1) * LANES]
            s = _dot_nt(q, k)
            ck = crow_ref[j]
            bias = jnp.concatenate([cq[hh] - ck[2 * p + hh:2 * p + hh + 1, :] for hh in range(2)], axis=0)
            kpos = ks + lax.broadcasted_iota(jnp.int32, (1, tk), 1)
            s = jnp.where(kpos <= qpos2, s + bias, MASKED)
            _online_update(s, v, m_ref, l_ref, acc_ref)
            return carry

        lax.fori_loop(0, n_tiles, tile, 0)
        o = acc_ref[...] / l_ref[...]
        o_ref[:, p * LANES:(p + 1) * LANES] = jnp.where(lane < HEAD_DIM, o[:tq], o[tq:]).astype(o_ref.dtype)


def _fox_prompt(qf, ccol, crow, fkvb, tq, tk):
    B, T, _ = qf.shape
    return pl.pallas_call(
        functools.partial(_fox_body, tq=tq, tk=tk),
        out_shape=jax.ShapeDtypeStruct((B, T, FOX_W), BF16),
        grid=(B, T // tq),
        in_specs=[
            pl.BlockSpec((None, tq, FOX_HEADS * LANES), lambda b, i: (b, i, 0)),
            pl.BlockSpec((None, tq, LANES), lambda b, i: (b, i, 0)),
            pl.BlockSpec((None, T // tk, 8, tk), lambda b, i: (b, 0, 0, 0)),
            pl.BlockSpec((None, T, 2 * FOX_W), lambda b, i: (b, 0, 0)),
        ],
        out_specs=pl.BlockSpec((None, tq, FOX_W), lambda b, i: (b, i, 0)),
        scratch_shapes=[pltpu.VMEM((2 * tq, 1), F32), pltpu.VMEM((2 * tq, 1), F32),
                        pltpu.VMEM((2 * tq, LANES), F32)],
        compiler_params=_cparams(2),
        name="fox_attention_prompt",
    )(qf, ccol, crow, fkvb)


def _masked_softmax_rows(s, valid):
    s = jnp.where(valid, s, MASKED)
    m = jnp.max(s, axis=-1, keepdims=True)
    e = jnp.where(valid, jnp.exp(s - m), 0.0)
    d = jnp.sum(e, axis=-1, keepdims=True)
    return e / jnp.where(d > 0, d, 1.0)


def _importance_matrix(nc, ns):
    ratio = SEL_LEN // CMP_STRIDE
    m = np.zeros((nc, ns), np.float32)
    for j in range(ns):
        for i in range(ratio * j, ratio * j + ratio):
            for src in (i, i - 1):
                if 0 <= src < nc:
                    m[src, j] += 1.0
    return m


def _topk_mask(score, n_pick):
    lane = lax.broadcasted_iota(jnp.int32, score.shape, 1)
    width = score.shape[1]
    picked = jnp.zeros(score.shape, jnp.bool_)
    for _ in range(n_pick):
        m = jnp.max(score, axis=-1, keepdims=True)
        first = jnp.min(jnp.where(score == m, lane, width), axis=-1, keepdims=True)
        hit = lane == first
        picked = picked | hit
        score = jnp.where(hit, -jnp.inf, score)
    return picked


def _nsa_body(q_ref, sm_ref, kcvc_ref, sel_ref, win_ref, impm_ref, o_ref, m_ref, l_ref, acc_ref, *, tq, tk, wk):
    i = pl.program_id(1)
    t0 = i * tq
    T = sel_ref.shape[0]
    nc = kcvc_ref.shape[0]
    ns = impm_ref.shape[1]
    R = NSA_GROUP
    rows = R * tq
    n_tiles = (t0 + tq + tk - 1) // tk
    qpos = t0 + lax.broadcasted_iota(jnp.int32, (tq, 1), 0)
    qpos_r = jnp.concatenate([qpos] * R, axis=0)
    lane_o = lax.broadcasted_iota(jnp.int32, (rows, LANES), 1)
    gates = sm_ref[...]
    out = None
    for g in range(NSA_KV_HEADS):
        q = jnp.concatenate([q_ref[:, (g * R + r) * LANES:(g * R + r + 1) * LANES] for r in range(R)], axis=0)
        slope = jnp.concatenate(
            [jnp.full((tq, 1), 2.0 ** (-8.0 * (g * R + r + 1) / NSA_HEADS), F32) for r in range(R)], axis=0)

        c_end = lax.broadcasted_iota(jnp.int32, (1, nc), 1) * CMP_STRIDE + (CMP_LEN - 1)
        dist_c = qpos_r - c_end
        s_c = _dot_nt(q, kcvc_ref[:, 0:LANES]) - slope * dist_c.astype(F32)
        p_c = _masked_softmax_rows(s_c, dist_c >= 0)
        o_c = _dot(p_c.astype(BF16), kcvc_ref[:, LANES:2 * LANES])

        imp = p_c[0:tq]
        for r in range(1, R):
            imp = imp + p_c[r * tq:(r + 1) * tq]
        imp_sel = jnp.zeros((tq, ns), F32)
        for part in _split3(imp):
            imp_sel = imp_sel + _dot(part, impm_ref[...])
        blk = lax.broadcasted_iota(jnp.int32, (tq, ns), 1)
        cur = qpos // SEL_LEN
        forced = (blk == 0) | (blk == cur) | (blk == cur - 1)
        score = jnp.where(forced, FORCE_SCORE, imp_sel)
        score = jnp.where(blk > cur, -3.0e38, score)
        picked = _topk_mask(score, min(SEL_TOPK, ns)) & (blk <= cur)
        picked_bf = jnp.where(picked, 1.0, 0.0).astype(BF16)

        m_ref[...] = jnp.full(m_ref.shape, MASKED, F32)
        l_ref[...] = jnp.zeros(l_ref.shape, F32)
        acc_ref[...] = jnp.zeros(acc_ref.shape, F32)

        def tile(j, carry, q=q, slope=slope, picked_bf=picked_bf):
            ks = pl.multiple_of(j * tk, tk)
            k = sel_ref[pl.ds(ks, tk), 0:LANES]
            v = sel_ref[pl.ds(ks, tk), LANES:2 * LANES]
            kidx = lax.broadcasted_iota(jnp.int32, (ns, tk), 1)
            bidx = lax.broadcasted_iota(jnp.int32, (ns, tk), 0)
            expand = jnp.where(bidx == j * (tk // SEL_LEN) + kidx // SEL_LEN, 1.0, 0.0).astype(BF16)
            chosen = _dot(picked_bf, expand)
            chosen = jnp.concatenate([chosen] * R, axis=0)
            kpos = ks + lax.broadcasted_iota(jnp.int32, (1, tk), 1)
            dist = qpos_r - kpos
            s = _dot_nt(q, k) - slope * dist.astype(F32)
            valid = (dist >= 0) & (chosen > 0.5)
            s = jnp.where(valid, s, MASKED)
            m_old = m_ref[...]
            m_new = jnp.maximum(m_old, jnp.max(s, axis=-1, keepdims=True))
            e = jnp.where(valid, jnp.exp(s - m_new), 0.0)
            alpha = jnp.exp(m_old - m_new)
            l_ref[...] = alpha * l_ref[...] + jnp.sum(e, axis=-1, keepdims=True)
            acc_ref[...] = alpha * acc_ref[...] + _dot(e.astype(BF16), v)
            m_ref[...] = m_new
            return carry

        lax.fori_loop(0, n_tiles, tile, 0)
        l_s = l_ref[...]
        o_s = acc_ref[...] / jnp.where(l_s > 0, l_s, 1.0)

        ws = pl.multiple_of(jnp.clip(t0 - WINDOW, 0, T - wk), LANES)
        kw = win_ref[pl.ds(ws, wk), 0:LANES]
        vw = win_ref[pl.ds(ws, wk), LANES:2 * LANES]
        dist_w = qpos_r - (ws + lax.broadcasted_iota(jnp.int32, (1, wk), 1))
        s_w = _dot_nt(q, kw) - slope * dist_w.astype(F32)
        p_w = _masked_softmax_rows(s_w, (dist_w >= 0) & (dist_w <= WINDOW))
        o_w = _dot(p_w.astype(BF16), vw)

        def gate(j):
            cols = [jnp.broadcast_to(gates[:, (g * R + r) * 3 + j:(g * R + r) * 3 + j + 1], (tq, LANES))
                    for r in range(R)]
            return jnp.concatenate(cols, axis=0)

        og = gate(0) * o_c + gate(1) * o_s + gate(2) * o_w
        out = og if out is None else jnp.where(lane_o < g * HEAD_DIM, out, og)
    for r in range(R):
        o_ref[r] = out[r * tq:(r + 1) * tq].astype(o_ref.dtype)


def _nsa_prompt(qn, sm, kcvc, nkvb, winb, tq, tk):
    B, T, _ = qn.shape
    nc = kcvc.shape[1]
    ns = T // SEL_LEN
    wk = WINDOW + tq
    impm = jnp.asarray(_importance_matrix(nc, ns), BF16)
    return pl.pallas_call(
        functools.partial(_nsa_body, tq=tq, tk=tk, wk=wk),
        out_shape=jax.ShapeDtypeStruct((B, NSA_GROUP, T, LANES), BF16),
        grid=(B, T // tq),
        in_specs=[
            pl.BlockSpec((None, tq, NSA_HEADS * LANES), lambda b, i: (b, i, 0)),
            pl.BlockSpec((None, tq, LANES), lambda b, i: (b, i, 0)),
            pl.BlockSpec((None, nc, 2 * NSA_KV), lambda b, i: (b, 0, 0)),
            pl.BlockSpec((None, T, 2 * NSA_KV), lambda b, i: (b, 0, 1)),
            pl.BlockSpec((None, T, 2 * NSA_KV), lambda b, i: (b, 0, 0)),
            _const_spec((nc, ns), lambda b, i: (0, 0)),
        ],
        out_specs=pl.BlockSpec((None, NSA_GROUP, tq, LANES), lambda b, i: (b, 0, i, 0)),
        scratch_shapes=[pltpu.VMEM((NSA_GROUP * tq, 1), F32), pltpu.VMEM((NSA_GROUP * tq, 1), F32),
                        pltpu.VMEM((NSA_GROUP * tq, LANES), F32)],
        compiler_params=_cparams(2),
        name="nsa_attention_prompt",
    )(qn, sm, kcvc, nkvb, winb, impm)


def _out_body(x_ref, yc_ref, on_ref, of_ref, w_ref, g_ref, o_ref):
    cat = jnp.concatenate([yc_ref[...]] + [on_ref[r] for r in range(NSA_GROUP)] + [of_ref[...]], axis=1)
    y = _dot(cat, w_ref[...])
    o_ref[...] = x_ref[...] + _rms(y, g_ref[...])


def _out_proj(x, yconv, onsa, ofox, w_out_p, norm_g, l, tm):
    B, T, D = x.shape
    return pl.pallas_call(
        _out_body,
        out_shape=jax.ShapeDtypeStruct(x.shape, F32),
        grid=(B, T // tm),
        in_specs=[
            pl.BlockSpec((None, tm, D), lambda b, i: (b, i, 0)),
            pl.BlockSpec((None, tm, CONV_CH), lambda b, i: (b, i, 0)),
            pl.BlockSpec((None, NSA_GROUP, tm, LANES), lambda b, i: (b, 0, i, 0)),
            pl.BlockSpec((None, tm, FOX_W), lambda b, i: (b, i, 0)),
            _const_spec((None, D, D), lambda b, i: (l, 0, 0)),
            _const_spec((None, None, 1, D), lambda b, i: (l, 3, 0, 0)),
        ],
        out_specs=pl.BlockSpec((None, tm, D), lambda b, i: (b, i, 0)),
        compiler_params=_cparams(2),
        name="output_projection",
    )(x, yconv, onsa, ofox, w_out_p, norm_g)


def _proj_column_index():
    in_cols = 2 * CONV_CH + NSA_Q + 6 * NSA_KV + N_GATES + 3 * FOX_W + FOX_HEADS
    src_qn = 2 * CONV_CH
    src_kv = src_qn + NSA_Q
    src_gate = src_kv + 6 * NSA_KV
    src_fox = src_gate + N_GATES
    src_ff = src_fox + 3 * FOX_W
    idx = np.full((PROJ_COLS,), in_cols, np.int32)
    idx[C_GLU:C_QN] = np.arange(2 * CONV_CH)
    for h in range(NSA_HEADS):
        g = h // NSA_GROUP
        base = C_QN + h * LANES + g * HEAD_DIM
        idx[base:base + HEAD_DIM] = src_qn + h * HEAD_DIM + np.arange(HEAD_DIM)
    idx[C_NKV:C_QF] = src_kv + np.arange(6 * NSA_KV)
    for h in range(FOX_HEADS):
        base = C_QF + h * LANES + (h % 2) * HEAD_DIM
        idx[base:base + HEAD_DIM] = src_fox + h * HEAD_DIM + np.arange(HEAD_DIM)
    idx[C_FKV:C_SM] = src_fox + FOX_W + np.arange(2 * FOX_W)
    idx[C_SM:C_SM + N_GATES] = src_gate + np.arange(N_GATES)
    idx[C_SM + N_GATES:C_SM + N_GATES + FOX_HEADS] = src_ff + np.arange(FOX_HEADS)
    return idx


def _out_row_index():
    idx = np.arange(D_MODEL, dtype=np.int32)
    for r in range(NSA_GROUP):
        for g in range(NSA_KV_HEADS):
            dst = CONV_CH + r * LANES + g * HEAD_DIM
            src = CONV_CH + (g * NSA_GROUP + r) * HEAD_DIM
            idx[dst:dst + HEAD_DIM] = src + np.arange(HEAD_DIM)
    return idx


def _pack_params(w_in, w_out, cmp_pe, cmp_w, nsa_gate_b, fox_forget_b):
    depth = w_in.shape[0]
    w_ext = jnp.concatenate([w_in, jnp.zeros((depth, D_MODEL, 1), w_in.dtype)], axis=-1)
    w_proj = jnp.take(w_ext, jnp.asarray(_proj_column_index()), axis=-1).astype(BF16)
    b_small = jnp.concatenate([nsa_gate_b, fox_forget_b,
                               jnp.zeros((depth, LANES - N_GATES - FOX_HEADS), F32)], axis=-1)[:, None, :]
    w_out_p = jnp.take(w_out, jnp.asarray(_out_row_index()), axis=1).astype(BF16)
    w3 = cmp_w.reshape(depth, 2, CMP_LEN, HEAD_DIM, HEAD_DIM)
    eye_g = jnp.eye(NSA_KV_HEADS, dtype=F32)
    blocks = [jnp.einsum('ab,xlde->xladbe', eye_g, w3[:, kv]).reshape(depth, CMP_LEN, NSA_KV, NSA_KV)
              for kv in range(2)]
    zero = jnp.zeros_like(blocks[0])
    w_cmp = jnp.concatenate([jnp.concatenate([blocks[0], zero], axis=-1),
                             jnp.concatenate([zero, blocks[1]], axis=-1)], axis=-2).astype(BF16)
    pe = jnp.concatenate([cmp_pe[:, 0], cmp_pe[:, 0], cmp_pe[:, 1], cmp_pe[:, 1]], axis=-1)
    return w_proj, b_small, w_out_p, w_cmp, pe[:, :, None, :]


def _mixer_prompt(x, norm_g, packed, conv_p, l, tiles):
    w_proj, b_small, w_out_p, w_cmp, pe_cmp = packed
    conv_w, conv_b, ln_g, ln_b = conv_p
    B, T, _ = x.shape
    u, qn, nkv, nkvb, win, winb, qf, fkv, fkvb, sm = _proj(x, norm_g, w_proj, b_small, l, tiles['tm'])
    yconv = _conv_prompt(u, conv_w, conv_b, ln_g, ln_b, l, tiles['tc'])
    kcvc = _compress_prompt(nkv, w_cmp, pe_cmp, l)
    onsa = _nsa_prompt(qn, sm, kcvc, nkvb, winb, tiles['nsa_tq'], tiles['nsa_tk'])
    ccol, crow = _cumsum(sm, tiles['fox_tk'])
    ofox = _fox_prompt(qf, ccol, crow, fkvb, tiles['fox_tq'], tiles['fox_tk'])
    mixed = _out_proj(x, yconv, onsa, ofox, w_out_p, norm_g, l, tiles['tm'])
    states = (nkv.reshape(B, T, 4, NSA_KV_HEADS, HEAD_DIM),
              win[:, T - WINDOW:].reshape(B, WINDOW, 2, NSA_KV_HEADS, HEAD_DIM),
              fkv.reshape(B, T, 2, FOX_HEADS, HEAD_DIM),
              sm[:, :, SM_LOGF:SM_LOGF + FOX_HEADS],
              u[:, T - (CONV_WIDTH - 1):])
    return mixed, states


def _row_to_col(row, first_lane, stride=1):
    lane = lax.broadcasted_iota(jnp.int32, (8, LANES), 1)
    sub = lax.broadcasted_iota(jnp.int32, (8, LANES), 0)
    picked = jnp.where(lane == first_lane + stride * sub, jnp.broadcast_to(row, (8, LANES)), 0.0)
    return jnp.sum(picked, axis=-1, keepdims=True)


def _rows8(blocks):
    n = blocks[0].shape[1]
    sub = lax.broadcasted_iota(jnp.int32, (8, n), 0)
    out = jnp.zeros((8, n), F32)
    for r, blk in enumerate(blocks):
        out = jnp.where(sub == r, jnp.broadcast_to(blk, (8, n)), out)
    return out


def _conv_sample_body(st_ref, u_ref, w_ref, b_ref, lng_ref, lnb_ref, o_ref):
    acc = u_ref[...] * w_ref[CONV_WIDTH - 1:CONV_WIDTH, :]
    for k in range(CONV_WIDTH - 1):
        acc = acc + st_ref[:, k, :] * w_ref[k:k + 1, :]
    o_ref[...] = _conv_post(acc, b_ref[...], lng_ref[...], lnb_ref[...]).astype(o_ref.dtype)


def _conv_sample(state_conv, u, conv_w, conv_b, ln_g, ln_b, l):
    nb, C = u.shape
    hist = CONV_WIDTH - 1
    return pl.pallas_call(
        _conv_sample_body,
        out_shape=jax.ShapeDtypeStruct((nb, C), BF16),
        grid=(1,),
        in_specs=[
            pl.BlockSpec((None, nb, hist, C), lambda i: (l, 0, 0, 0)),
            pl.BlockSpec((nb, C), lambda i: (0, 0)),
            pl.BlockSpec((None, CONV_WIDTH, C), lambda i: (l, 0, 0)),
            pl.BlockSpec((None, 1, C), lambda i: (l, 0, 0)),
            pl.BlockSpec((None, 1, C), lambda i: (l, 0, 0)),
            pl.BlockSpec((None, 1, C), lambda i: (l, 0, 0)),
        ],
        out_specs=pl.BlockSpec((nb, C), lambda i: (0, 0)),
        compiler_params=_cparams(1),
        name="conv_module_sample",
    )(state_conv, u, conv_w, conv_b, ln_g, ln_b)


def _compress_sample_body(pt_ref, *refs, pg):
    pages_k = refs[:pg]
    pages_v = refs[pg:2 * pg]
    w_ref, pe_ref, o_ref = refs[2 * pg:]
    rows = PAGE_SIZE // CMP_STRIDE
    n = pg * rows

    def load(l):
        halves = [jnp.concatenate([p[pl.ds(l, rows, stride=CMP_STRIDE), :] for p in pages], axis=0)
                  for pages in (pages_k, pages_v)]
        return jnp.concatenate(halves, axis=1)

    first, second = _compress_halves(load, w_ref, pe_ref, n)
    o_ref[:, 0:2 * NSA_KV] = first
    o_ref[:, 2 * NSA_KV:4 * NSA_KV] = second


def _compress_sample(cache_nsa, page_table, w_cmp, pe_cmp, l, pg):
    nb, n_pages = page_table.shape
    rows = PAGE_SIZE // CMP_STRIDE
    page_specs = [
        pl.BlockSpec((None, None, PAGE_SIZE, NSA_KV),
                     lambda b, j, pt, i=i, half=half: (l, pt[b, j * pg + i], 0, half))
        for half in range(2) for i in range(pg)]
    return pl.pallas_call(
        functools.partial(_compress_sample_body, pg=pg),
        out_shape=jax.ShapeDtypeStruct((nb, n_pages * rows, 4 * NSA_KV), F32),
        grid_spec=pltpu.PrefetchScalarGridSpec(
            num_scalar_prefetch=1,
            grid=(nb, n_pages // pg),
            in_specs=page_specs + [
                _const_spec((None, CMP_LEN, 2 * NSA_KV, 2 * NSA_KV), lambda b, j, pt: (l, 0, 0, 0)),
                _const_spec((None, CMP_LEN, 1, 2 * NSA_KV), lambda b, j, pt: (l, 0, 0, 0)),
            ],
            out_specs=pl.BlockSpec((None, pg * rows, 4 * NSA_KV), lambda b, j, pt: (b, j, 0)),
        ),
        compiler_params=_cparams(2),
        name="nsa_compress_sample",
    )(page_table, *([cache_nsa] * (2 * pg)), w_cmp, pe_cmp)


def _sample_q(q_ref, g):
    rows = [q_ref[:, (g * NSA_GROUP + r) * LANES:(g * NSA_GROUP + r + 1) * LANES].astype(F32)
            for r in range(NSA_GROUP)]
    return _rows8(rows).astype(BF16)


def _sample_slopes(g):
    sub = lax.broadcasted_iota(jnp.int32, (8, 1), 0)
    slope = jnp.zeros((8, 1), F32)
    for r in range(NSA_GROUP):
        slope = jnp.where(sub == r, 2.0 ** (-8.0 * (g * NSA_GROUP + r + 1) / NSA_HEADS), slope)
    return slope


def _nsa_sample_a_body(q_ref, sm_ref, fs_ref, win_ref, winn_ref, impm_ref, part_ref, gsel_ref, pick_ref,
                       *, past, ns):
    nc = fs_ref.shape[0]
    wb = win_ref.shape[0]
    nsp = impm_ref.shape[1]
    lane = lax.broadcasted_iota(jnp.int32, (8, LANES), 1)
    sub = lax.broadcasted_iota(jnp.int32, (8, LANES), 0)
    kcvc = (fs_ref[:, 0:2 * NSA_KV] + pltpu.roll(fs_ref[:, 2 * NSA_KV:4 * NSA_KV], nc - 1, axis=0)).astype(BF16)
    new_rows = lax.broadcasted_iota(jnp.int32, (LANES, 2 * NSA_KV), 0)
    win_new = jnp.where(new_rows == 0, jnp.broadcast_to(winn_ref[...], (LANES, 2 * NSA_KV)), 0.0)
    win_all = jnp.concatenate([win_ref[...], win_new], axis=0).astype(BF16)
    sm_row = sm_ref[...]
    part = None
    gsel = None
    picks = jnp.zeros((8, LANES), jnp.int32)
    for g in range(NSA_KV_HEADS):
        q = _sample_q(q_ref, g)
        slope = _sample_slopes(g)
        c_end = lax.broadcasted_iota(jnp.int32, (1, nc), 1) * CMP_STRIDE + (CMP_LEN - 1)
        dist_c = past - c_end
        s_c = _dot_nt(q, kcvc[:, 0:LANES]) - slope * dist_c.astype(F32)
        p_c = _masked_softmax_rows(s_c, jnp.broadcast_to(dist_c >= 0, s_c.shape))
        o_c = _dot(p_c.astype(BF16), kcvc[:, LANES:2 * LANES])
        sub_c = lax.broadcasted_iota(jnp.int32, p_c.shape, 0)
        imp = jnp.sum(jnp.where(sub_c < NSA_GROUP, p_c, 0.0), axis=0, keepdims=True)
        imp = jnp.broadcast_to(imp, p_c.shape)
        imp_sel = jnp.zeros((8, nsp), F32)
        for piece in _split3(imp):
            imp_sel = imp_sel + _dot(piece, impm_ref[...])
        blk = lax.broadcasted_iota(jnp.int32, (8, nsp), 1)
        cur = past // SEL_LEN
        forced = (blk == 0) | (blk == cur) | (blk == cur - 1)
        score = jnp.where(forced, FORCE_SCORE, imp_sel)
        score = jnp.where((blk > cur) | (blk >= ns), -3.0e38, score)
        blk_f = blk.astype(F32)
        for it in range(min(SEL_TOPK, ns)):
            m = jnp.max(score, axis=-1, keepdims=True)
            first = jnp.min(jnp.where(score == m, blk_f, float(nsp)), axis=-1, keepdims=True)
            score = jnp.where(blk_f == first, -jnp.inf, score)
            picks = jnp.where((lane == it) & (sub == g), first.astype(jnp.int32), picks)
        widx = lax.broadcasted_iota(jnp.int32, (1, wb + LANES), 1)
        dist_w = jnp.where(widx < wb, wb - widx, 0)
        s_w = _dot_nt(q, win_all[:, 0:LANES]) - slope * dist_w.astype(F32)
        p_w = _masked_softmax_rows(s_w, jnp.broadcast_to(widx <= wb, s_w.shape))
        o_w = _dot(p_w.astype(BF16), win_all[:, LANES:2 * LANES])
        g_c = _row_to_col(sm_row, g * NSA_GROUP * 3 + 0, 3)
        g_s = _row_to_col(sm_row, g * NSA_GROUP * 3 + 1, 3)
        g_w = _row_to_col(sm_row, g * NSA_GROUP * 3 + 2, 3)
        pg_ = g_c * o_c + g_w * o_w
        gs_ = jnp.broadcast_to(g_s, (8, LANES))
        part = pg_ if part is None else jnp.where(lane < g * HEAD_DIM, part, pg_)
        gsel = gs_ if gsel is None else jnp.where(lane < g * HEAD_DIM, gsel, gs_)
    part_ref[...] = part
    gsel_ref[...] = gsel
    pick_ref[...] = picks


def _nsa_sample_a(qn, sm, fs, state_win, win_new, l, past):
    nb = qn.shape[0]
    nc = fs.shape[1]
    wb = state_win.shape[2]
    ns = -(-(past + 1) // SEL_LEN)
    nsp = -(-ns // LANES) * LANES
    impm = jnp.asarray(_importance_matrix(nc, nsp), BF16)
    row = lambda w: pl.BlockSpec((None, 1, w), lambda b: (b, 0, 0))
    tile = pl.BlockSpec((None, 8, LANES), lambda b: (b, 0, 0))
    return pl.pallas_call(
        functools.partial(_nsa_sample_a_body, past=past, ns=ns),
        out_shape=[jax.ShapeDtypeStruct((nb, 8, LANES), F32), jax.ShapeDtypeStruct((nb, 8, LANES), F32),
                   jax.ShapeDtypeStruct((nb, 8, LANES), jnp.int32)],
        grid=(nb,),
        in_specs=[
            row(NSA_HEADS * LANES), row(LANES),
            pl.BlockSpec((None, nc, 4 * NSA_KV), lambda b: (b, 0, 0)),
            pl.BlockSpec((None, None, wb, 2 * NSA_KV), lambda b: (l, b, 0, 0)),
            row(2 * NSA_KV),
            _const_spec((nc, nsp), lambda b: (0, 0)),
        ],
        out_specs=[tile, tile, tile],
        compiler_params=_cparams(1),
        name="nsa_sample_compressed_window",
    )(qn, sm, fs, state_win, win_new, impm)


def _nsa_sample_b_body(pt_ref, pick_ref, *refs, past, n_pick):
    blocks = refs[:NSA_KV_HEADS * n_pick]
    q_ref, new_ref, part_ref, gsel_ref, o_ref = refs[NSA_KV_HEADS * n_pick:]
    b = pl.program_id(0)
    lane = lax.broadcasted_iota(jnp.int32, (8, LANES), 1)
    n_keys = n_pick * SEL_LEN
    new_rows = lax.broadcasted_iota(jnp.int32, (LANES, 2 * NSA_KV), 0)
    kv_new = jnp.where(new_rows == 0, jnp.broadcast_to(new_ref[:, 2 * NSA_KV:4 * NSA_KV], (LANES, 2 * NSA_KV)), 0.0)
    kidx = lax.broadcasted_iota(jnp.int32, (1, n_keys + LANES), 1)
    o_s = None
    for g in range(NSA_KV_HEADS):
        q = _sample_q(q_ref, g)
        slope = _sample_slopes(g)
        kv = jnp.concatenate([blocks[g * n_pick + i][...] for i in range(n_pick)] + [kv_new], axis=0).astype(BF16)
        kpos = jnp.where(kidx == n_keys, past, past + 1)
        for i in range(n_pick):
            start = pick_ref[b, g * n_pick + i] * SEL_LEN
            kpos = jnp.where(kidx // SEL_LEN == i, start + kidx % SEL_LEN, kpos)
        dist = past - kpos
        valid = (dist >= 0) & ((dist > 0) | (kidx == n_keys))
        s = _dot_nt(q, kv[:, 0:LANES]) - slope * dist.astype(F32)
        p = _masked_softmax_rows(s, jnp.broadcast_to(valid, s.shape))
        og = _dot(p.astype(BF16), kv[:, LANES:2 * LANES])
        o_s = og if o_s is None else jnp.where(lane < g * HEAD_DIM, o_s, og)
    o_ref[...] = (part_ref[...] + gsel_ref[...] * o_s).astype(o_ref.dtype)


def _nsa_sample_b(cache_sel, page_table, picks, qn, nkv_new, part, gsel, l, past, n_pick):
    nb, n_pages = page_table.shape
    halves = PAGE_SIZE // SEL_LEN

    def block_index(b, pt, pk, i):
        j = pk[b, i]
        page = pt[b, jnp.minimum(j // halves, n_pages - 1)]
        return (l, page * halves + j % halves, 0, 1)

    blk_specs = [pl.BlockSpec((None, None, SEL_LEN, 2 * NSA_KV), functools.partial(
        lambda b, pt, pk, i: block_index(b, pt, pk, i), i=i)) for i in range(NSA_KV_HEADS * n_pick)]
    row = lambda w: pl.BlockSpec((None, 1, w), lambda b, pt, pk: (b, 0, 0))
    tile = pl.BlockSpec((None, 8, LANES), lambda b, pt, pk: (b, 0, 0))
    return pl.pallas_call(
        functools.partial(_nsa_sample_b_body, past=past, n_pick=n_pick),
        out_shape=jax.ShapeDtypeStruct((nb, 8, LANES), BF16),
        grid_spec=pltpu.PrefetchScalarGridSpec(
            num_scalar_prefetch=2,
            grid=(nb,),
            in_specs=blk_specs + [row(NSA_HEADS * LANES), row(4 * NSA_KV), tile, tile],
            out_specs=tile,
        ),
        compiler_params=_cparams(1),
        name="nsa_sample_selected",
    )(page_table, picks, *([cache_sel] * (NSA_KV_HEADS * n_pick)), qn, nkv_new, part, gsel)


def _fox_sample_body(pt_ref, *refs, pg):
    pages = refs[:pg]
    lfs = refs[pg:2 * pg]
    q_ref, new_ref, sm_ref, ust_ref, aft_ref, o_ref, m_ref, l_ref, acc_ref, carry_ref = refs[2 * pg:]
    jj = pl.program_id(1)
    lane2 = lax.broadcasted_iota(jnp.int32, (8, FOX_W), 1)
    sub2 = lax.broadcasted_iota(jnp.int32, (8, FOX_W), 0)
    heads = [q_ref[:, h * LANES:(h + 1) * LANES].astype(F32) for h in range(FOX_HEADS)]
    zero = jnp.zeros((1, LANES), F32)
    q = _rows8([jnp.concatenate([heads[h] if h // 2 == p else zero for p in range(FOX_HEADS // 2)], axis=1)
                for h in range(FOX_HEADS)]).astype(BF16)

    @pl.when(jj == 0)
    def _():
        k_new = jnp.where(sub2 == 0, jnp.broadcast_to(new_ref[:, 0:FOX_W], (8, FOX_W)), 0.0).astype(BF16)
        s_new = _dot_nt(q, k_new)[:, 0:1]
        m_ref[...] = s_new
        l_ref[...] = jnp.ones(l_ref.shape, F32)
        acc_ref[...] = jnp.broadcast_to(new_ref[:, FOX_W:2 * FOX_W].astype(BF16).astype(F32), (8, FOX_W))
        carry_ref[...] = _row_to_col(sm_ref[...], SM_LOGF)

    lf = jnp.concatenate([r[...] for r in lfs], axis=0)
    ones = jnp.ones((LANES, LANES), BF16)
    within = jnp.zeros(lf.shape, F32)
    total = jnp.zeros(lf.shape, F32)
    for piece in _split3(lf):
        within = within + _dot(piece, ust_ref[...])
        total = total + _dot(piece, ones)
    after = jnp.zeros(lf.shape, F32)
    for piece in _split3(total):
        after = after + _dot(aft_ref[...], piece)
    carry = carry_ref[...]
    bias = within + after + jnp.concatenate([carry] * pg, axis=0)
    s = jnp.concatenate([_dot_nt(q, p[:, 0:FOX_W].astype(BF16)) for p in pages], axis=0) + bias
    s3 = s.reshape(pg, 8, LANES)
    m_old = m_ref[...]
    m_new = jnp.maximum(m_old, jnp.max(jnp.max(s3, axis=0), axis=-1, keepdims=True))
    e3 = jnp.exp(s3 - m_new[None])
    alpha = jnp.exp(m_old - m_new)
    l_ref[...] = alpha * l_ref[...] + jnp.sum(jnp.sum(e3, axis=0), axis=-1, keepdims=True)
    acc = alpha * acc_ref[...]
    for i in range(pg):
        acc = acc + _dot(e3[i].astype(BF16), pages[i][:, FOX_W:2 * FOX_W].astype(BF16))
    acc_ref[...] = acc
    m_ref[...] = m_new
    carry_ref[...] = bias[0:8, 0:1] + lf[0:8, 0:1]

    @pl.when(jj == pl.num_programs(1) - 1)
    def _():
        o = acc_ref[...] / l_ref[...]
        o_ref[...] = jnp.sum(jnp.where(lane2 // HEAD_DIM == sub2, o, 0.0), axis=0, keepdims=True).astype(o_ref.dtype)


def _fox_sample(cache_fox, lft, page_table, qf, fkv_new, sm, l, pg):
    nb, n_pages = page_table.shape
    steps = n_pages // pg
    ust = jnp.asarray(np.triu(np.ones((LANES, LANES), np.float32), 1).T, BF16)
    idx = np.arange(pg * 8)
    aft = ((idx[:, None] % 8 == idx[None, :] % 8) & (idx[None, :] // 8 > idx[:, None] // 8)).astype(np.float32)

    def page_of(b, j, pt, i):
        return pt[b, (steps - 1 - j) * pg + i]

    page_specs = [pl.BlockSpec((None, None, PAGE_SIZE, 2 * FOX_W),
                               lambda b, j, pt, i=i: (l, page_of(b, j, pt, i), 0, 0)) for i in range(pg)]
    lf_specs = [pl.BlockSpec((None, None, 8, PAGE_SIZE),
                             lambda b, j, pt, i=i: (l, page_of(b, j, pt, i), 0, 0)) for i in range(pg)]
    row = lambda w: pl.BlockSpec((None, 1, w), lambda b, j, pt: (b, 0, 0))
    return pl.pallas_call(
        functools.partial(_fox_sample_body, pg=pg),
        out_shape=jax.ShapeDtypeStruct((nb, 1, FOX_W), BF16),
        grid_spec=pltpu.PrefetchScalarGridSpec(
            num_scalar_prefetch=1,
            grid=(nb, steps),
            in_specs=page_specs + lf_specs + [
                row(FOX_HEADS * LANES), row(2 * FOX_W), row(LANES),
                _const_spec((LANES, LANES), lambda b, j, pt: (0, 0)),
                _const_spec((pg * 8, pg * 8), lambda b, j, pt: (0, 0)),
            ],
            out_specs=row(FOX_W),
            scratch_shapes=[pltpu.VMEM((8, 1), F32), pltpu.VMEM((8, 1), F32),
                            pltpu.VMEM((8, FOX_W), F32), pltpu.VMEM((8, 1), F32)],
        ),
        compiler_params=_cparams(2),
        name="fox_attention_sample",
    )(page_table, *([cache_fox] * pg), *([lft] * pg), qf, fkv_new, sm, ust, jnp.asarray(aft, BF16))


def _mixer_sample(x, norm_g, packed, conv_p, caches, page_table, l, tiles):
    w_proj, b_small, w_out_p, w_cmp, pe_cmp = packed
    conv_w, conv_b, ln_g, ln_b = conv_p
    cache_nsa, cache_sel, state_win, cache_fox, lft, state_conv = caches
    nb = x.shape[1]
    past = page_table.shape[1] * PAGE_SIZE
    u, qn, nkv, _, win, _, qf, fkv, _, sm = _proj(x, norm_g, w_proj, b_small, l, nb)
    per_seq = lambda a: a.reshape(nb, 1, a.shape[-1])
    yconv = _conv_sample(state_conv, u[0], conv_w, conv_b, ln_g, ln_b, l)
    fs = _compress_sample(cache_nsa, page_table, w_cmp, pe_cmp, l, tiles['cmp_pg'])
    part, gsel, picks = _nsa_sample_a(per_seq(qn), per_seq(sm), fs, state_win, per_seq(win), l, past)
    n_pick = min(SEL_TOPK, -(-(past + 1) // SEL_LEN))
    picks2 = picks[:, :NSA_KV_HEADS, :n_pick].reshape(nb, NSA_KV_HEADS * n_pick)
    onsa = _nsa_sample_b(cache_sel, page_table, picks2, per_seq(qn), per_seq(nkv), part, gsel, l, past, n_pick)
    onsa = jnp.transpose(onsa[:, :NSA_GROUP], (1, 0, 2))[None]
    ofox = _fox_sample(cache_fox, lft, page_table, per_seq(qf), per_seq(fkv), per_seq(sm), l, tiles['fox_pg'])
    mixed = _out_proj(x, yconv[None], onsa, ofox.reshape(1, nb, FOX_W), w_out_p, norm_g, l, nb)
    wb = state_win.shape[2]
    win_state = jnp.concatenate([state_win[l], win[0][:, None, :]], axis=1)[:, 1:]
    conv_state = jnp.concatenate([state_conv[l], u[0][:, None, :]], axis=1)[:, 1:]
    states = (nkv.reshape(nb, 1, 4, NSA_KV_HEADS, HEAD_DIM),
              win_state.reshape(nb, wb, 2, NSA_KV_HEADS, HEAD_DIM),
              fkv.reshape(nb, 1, 2, FOX_HEADS, HEAD_DIM),
              sm[0][:, None, SM_LOGF:SM_LOGF + FOX_HEADS],
              conv_state)
    return mixed, states


def _tiles(T, n_pages):
    return dict(tm=min(512, T), tc=min(512, T), nsa_tq=128, nsa_tk=min(512, T), fox_tq=min(256, T),
                fox_tk=min(512, T), cmp_pg=min(16, n_pages), fox_pg=min(16, n_pages))


def kernel(x_prompt, x_sample, cache_nsa_kv, state_nsa_win, cache_fox_kv, cache_fox_logf, state_conv,
           page_table, norm_g, ffn_w_gate, ffn_w_up, ffn_w_down, w_in, w_out, conv_w, conv_b,
           conv_ln_g, conv_ln_b, cmp_pe, cmp_w, nsa_gate_b, fox_forget_b):
    depth = w_in.shape[0]
    B, T, D = x_prompt.shape
    nb, ts, _ = x_sample.shape
    n_phys = cache_nsa_kv.shape[1]
    n_pages = page_table.shape[1]
    wb = state_nsa_win.shape[2]
    assert ts == 1 and D == D_MODEL and T >= WINDOW + 128 and wb == min(WINDOW, n_pages * PAGE_SIZE)
    tiles = _tiles(T, n_pages)
    assert n_pages % tiles['cmp_pg'] == 0 and n_pages % tiles['fox_pg'] == 0
    assert PAGE_SIZE % SEL_LEN == 0 and T % tiles['tm'] == 0 and T % tiles['fox_tk'] == 0

    packed = _pack_params(w_in, w_out, cmp_pe, cmp_w, nsa_gate_b, fox_forget_b)
    wg, wu, wd = (w.astype(BF16) for w in (ffn_w_gate, ffn_w_up, ffn_w_down))
    ng = norm_g[:, :, None, :]
    conv_p = (conv_w, conv_b[:, None, :], conv_ln_g[:, None, :], conv_ln_b[:, None, :])
    cache_nsa = cache_nsa_kv.reshape(depth, n_phys, PAGE_SIZE, 4 * NSA_KV)
    cache_sel = cache_nsa_kv.reshape(depth, n_phys * (PAGE_SIZE // SEL_LEN), SEL_LEN, 4 * NSA_KV)
    cache_fox = cache_fox_kv.reshape(depth, n_phys, PAGE_SIZE, 2 * FOX_W)
    lft = jnp.pad(jnp.swapaxes(cache_fox_logf, 2, 3), ((0, 0), (0, 0), (0, 8 - FOX_HEADS), (0, 0)))
    caches = (cache_nsa, cache_sel, state_nsa_win.reshape(depth, nb, wb, 2 * NSA_KV), cache_fox, lft, state_conv)

    xp = x_prompt
    xs = x_sample.reshape(1, nb, D)
    p_states, s_states = [], []
    for l in range(depth):
        xp = _ffn(xp, ng, wg, wu, wd, l, 0, tiles['tm'])
        xs = _ffn(xs, ng, wg, wu, wd, l, 0, nb)
        xp, ps = _mixer_prompt(xp, ng, packed, conv_p, l, tiles)
        xs, ss = _mixer_sample(xs, ng, packed, conv_p, caches, page_table, l, tiles)
        xp = _ffn(xp, ng, wg, wu, wd, l, 1, tiles['tm'])
        xs = _ffn(xs, ng, wg, wu, wd, l, 1, nb)
        p_states.append(ps)
        s_states.append(ss)
    stack = lambda states, k: jnp.stack([s[k] for s in states])
    return (xp, xs.reshape(nb, 1, D),
            *(stack(p_states, k) for k in range(5)),
            *(stack(s_states, k) for k in range(5)))
```

```python
import functools

import jax
import jax.numpy as jnp
import numpy as np
from jax import lax
from jax.experimental import pallas as pl
from jax.experimental.pallas import tpu as pltpu

F32 = jnp.float32
BF16 = jnp.bfloat16

D_MODEL = 1024
D_FF = 2816
HEAD_DIM = 64
CONV_CH = 256
CONV_WIDTH = 31
NSA_HEADS = 8
NSA_KV_HEADS = 2
NSA_GROUP = NSA_HEADS // NSA_KV_HEADS
FOX_HEADS = 4
CMP_LEN = 32
CMP_STRIDE = 16
SEL_LEN = 64
SEL_TOPK = 16
WINDOW = 512
PAGE_SIZE = 128
RMS_EPS = 1e-6
LN_EPS = 1e-5
FORCE_SCORE = 1e9

NSA_Q = NSA_HEADS * HEAD_DIM
NSA_KV = NSA_KV_HEADS * HEAD_DIM
FOX_W = FOX_HEADS * HEAD_DIM
N_GATES = 3 * NSA_HEADS

LANES = 128
V7X_VMEM_BYTES = 64 * 1024 * 1024
VMEM_LIMIT = (V7X_VMEM_BYTES * 7) // 8

MASKED = -1e30
FF_CHUNK = 256
CONV_HALO = 32

C_GLU = 0
C_QN = C_GLU + 2 * CONV_CH
C_NKV = C_QN + NSA_Q
C_WIN = C_NKV + 4 * NSA_KV
C_QF = C_WIN + 2 * NSA_KV
C_FKV = C_QF + FOX_W
C_SM = C_FKV + 2 * FOX_W
PROJ_COLS = C_SM + LANES
SM_LOGF = N_GATES

AUX = HEAD_DIM

NT_DIMS = (((1,), (1,)), ((), ()))


def _cparams(n_grid):
    return pltpu.CompilerParams(dimension_semantics=("arbitrary",) * n_grid,
                                vmem_limit_bytes=VMEM_LIMIT)


def _const_spec(shape, index):
    return pl.BlockSpec(shape, index, pipeline_mode=pl.Buffered(1))


def _rms(x, g):
    return x * lax.rsqrt(jnp.mean(x * x, axis=-1, keepdims=True) + RMS_EPS) * g


def _split3(x):
    hi = x.astype(BF16)
    r1 = x - hi.astype(F32)
    mid = r1.astype(BF16)
    lo = (r1 - mid.astype(F32)).astype(BF16)
    return hi, mid, lo


def _dot(a, b):
    return jnp.dot(a, b, preferred_element_type=F32)


def _dot_nt(a, b):
    return lax.dot_general(a, b, NT_DIMS, preferred_element_type=F32)


def _ffn_math(x, gpre, gpost, wg_ref, wu_ref, wd_ref):
    xn = _rms(x, gpre).astype(BF16)
    acc = jnp.zeros(x.shape, F32)
    for c in range(D_FF // FF_CHUNK):
        sl = slice(c * FF_CHUNK, (c + 1) * FF_CHUNK)
        g = _dot(xn, wg_ref[:, sl])
        u = _dot(xn, wu_ref[:, sl])
        h = (g * jax.nn.sigmoid(g) * u).astype(BF16)
        acc = acc + _dot(h, wd_ref[sl, :])
    return x + 0.5 * _rms(acc, gpost)


def _ffn_body(x_ref, gpre_ref, gpost_ref, wg_ref, wu_ref, wd_ref, o_ref):
    o_ref[...] = _ffn_math(x_ref[...], gpre_ref[...], gpost_ref[...], wg_ref, wu_ref, wd_ref)


def _ffn(x, norm_g, wg, wu, wd, l, k, tm):
    B, T, D = x.shape
    gi = 4 * k
    return pl.pallas_call(
        _ffn_body,
        out_shape=jax.ShapeDtypeStruct(x.shape, F32),
        grid=(B, T // tm),
        in_specs=[
            pl.BlockSpec((None, tm, D), lambda b, i: (b, i, 0)),
            _const_spec((None, None, 1, D), lambda b, i: (l, gi, 0, 0)),
            _const_spec((None, None, 1, D), lambda b, i: (l, gi + 1, 0, 0)),
            _const_spec((None, None, D, D_FF), lambda b, i: (l, k, 0, 0)),
            _const_spec((None, None, D, D_FF), lambda b, i: (l, k, 0, 0)),
            _const_spec((None, None, D_FF, D), lambda b, i: (l, k, 0, 0)),
        ],
        out_specs=pl.BlockSpec((None, tm, D), lambda b, i: (b, i, 0)),
        compiler_params=_cparams(2),
        name="ffn_half_step",
    )(x, norm_g, norm_g, wg, wu, wd)


def _alibi_slope(head):
    return 2.0 ** (-8.0 * (head + 1) / NSA_HEADS)


def _store_heads(o_ref, z, fills):
    lane = lax.broadcasted_iota(jnp.int32, (z.shape[0], LANES), 1)
    for h, fill in enumerate(fills):
        blk = z[:, (h // 2) * LANES:(h // 2 + 1) * LANES]
        if h % 2:
            blk = pltpu.roll(blk, HEAD_DIM, axis=1)
        o_ref[:, h * LANES:(h + 1) * LANES] = jnp.where(lane < HEAD_DIM, blk, fill).astype(o_ref.dtype)


def _position_fill(pos, lane):
    within = (pos % SEL_LEN).astype(F32)
    block = (pos - pos % SEL_LEN).astype(F32)
    return jnp.where(lane == AUX, within, jnp.where(lane == AUX + 1, block, 0.0))


def _proj_body(x_ref, g_ref, w_ref, b_ref, u_ref, qn_ref, nkv_ref, ksel_ref, vsel_ref, win_ref, kwin_ref,
               vwin_ref, qf_ref, fkv_ref, kf_ref, vf_ref, sm_ref):
    xn = _rms(x_ref[...], g_ref[...]).astype(BF16)
    tm = xn.shape[0]
    lane = lax.broadcasted_iota(jnp.int32, (tm, LANES), 1)
    pos = pl.program_id(1) * tm + lax.broadcasted_iota(jnp.int32, (tm, LANES), 0)
    pos_fill = _position_fill(pos, lane)
    ones_fill = jnp.ones((tm, LANES), F32)
    zero_fill = jnp.zeros((tm, LANES), F32)

    def mm(lo, hi):
        return _dot(xn, w_ref[:, lo:hi])

    z = mm(C_GLU, C_QN)
    u_ref[...] = z[:, :CONV_CH] * jax.nn.sigmoid(z[:, CONV_CH:])
    slope_fills = [jnp.where((lane == AUX) | (lane == AUX + 1), _alibi_slope(h), 0.0) for h in range(NSA_HEADS)]
    _store_heads(qn_ref, mm(C_QN, C_NKV) * (HEAD_DIM ** -0.5), slope_fills)
    z = mm(C_NKV, C_WIN)
    nkv_ref[...] = z
    _store_heads(ksel_ref, z[:, 2 * NSA_KV:3 * NSA_KV], [pos_fill] * NSA_KV_HEADS)
    _store_heads(vsel_ref, z[:, 3 * NSA_KV:4 * NSA_KV], [ones_fill] * NSA_KV_HEADS)
    z = mm(C_WIN, C_QF)
    win_ref[...] = z
    _store_heads(kwin_ref, z[:, 0:NSA_KV], [pos_fill] * NSA_KV_HEADS)
    _store_heads(vwin_ref, z[:, NSA_KV:2 * NSA_KV], [ones_fill] * NSA_KV_HEADS)
    one3 = jnp.where((lane >= AUX) & (lane < AUX + 3), 1.0, 0.0)
    _store_heads(qf_ref, mm(C_QF, C_FKV) * (HEAD_DIM ** -0.5), [one3] * FOX_HEADS)
    z = mm(C_FKV, C_SM)
    fkv_ref[...] = z
    _store_heads(kf_ref, z[:, 0:FOX_W], [zero_fill] * FOX_HEADS)
    _store_heads(vf_ref, z[:, FOX_W:2 * FOX_W], [ones_fill] * FOX_HEADS)
    z = mm(C_SM, PROJ_COLS) + b_ref[...]
    log_sig = jnp.minimum(z, 0.0) - jnp.log1p(jnp.exp(-jnp.abs(z)))
    sm_ref[...] = jnp.where(lane < N_GATES, jax.nn.sigmoid(z), log_sig)


def _proj(x, norm_g, w_proj, b_small, l, tm):
    B, T, D = x.shape
    widths = [(CONV_CH, F32), (NSA_HEADS * LANES, BF16), (4 * NSA_KV, F32), (NSA_KV_HEADS * LANES, BF16),
              (NSA_KV_HEADS * LANES, BF16), (2 * NSA_KV, F32), (NSA_KV_HEADS * LANES, BF16),
              (NSA_KV_HEADS * LANES, BF16), (FOX_HEADS * LANES, BF16), (2 * FOX_W, F32),
              (FOX_HEADS * LANES, BF16), (FOX_HEADS * LANES, BF16), (LANES, F32)]
    return pl.pallas_call(
        _proj_body,
        out_shape=[jax.ShapeDtypeStruct((B, T, w), dt) for w, dt in widths],
        grid=(B, T // tm),
        in_specs=[
            pl.BlockSpec((None, tm, D), lambda b, i: (b, i, 0)),
            _const_spec((None, None, 1, D), lambda b, i: (l, 2, 0, 0)),
            _const_spec((None, D, PROJ_COLS), lambda b, i: (l, 0, 0)),
            _const_spec((None, 1, LANES), lambda b, i: (l, 0, 0)),
        ],
        out_specs=[pl.BlockSpec((None, tm, w), lambda b, i: (b, i, 0)) for w, _ in widths],
        compiler_params=_cparams(2),
        name="input_projection",
    )(x, norm_g, w_proj, b_small)


def _conv_body(prev_ref, cur_ref, w_ref, b_ref, lng_ref, lnb_ref, o_ref, hist_ref, *, tc):
    i = pl.program_id(1)
    prev = prev_ref[...]
    hist_ref[0:CONV_HALO, :] = jnp.where(i == 0, 0.0, prev)
    hist_ref[CONV_HALO:CONV_HALO + tc, :] = cur_ref[...]
    off = CONV_HALO - (CONV_WIDTH - 1)
    acc = jnp.zeros((tc, CONV_CH), F32)
    for k in range(CONV_WIDTH):
        acc = acc + hist_ref[off + k:off + k + tc, :] * w_ref[k:k + 1, :]
    o_ref[...] = _conv_post(acc, b_ref[...], lng_ref[...], lnb_ref[...]).astype(o_ref.dtype)


def _conv_post(acc, b, ln_g, ln_b):
    y = acc + b
    mu = jnp.mean(y, axis=-1, keepdims=True)
    var = jnp.mean(jnp.square(y - mu), axis=-1, keepdims=True)
    y = (y - mu) * lax.rsqrt(var + LN_EPS) * ln_g + ln_b
    return y * jax.nn.sigmoid(y)


def _conv_prompt(u, conv_w, conv_b, ln_g, ln_b, l, tc):
    B, T, C = u.shape
    halo_blocks = tc // CONV_HALO
    return pl.pallas_call(
        functools.partial(_conv_body, tc=tc),
        out_shape=jax.ShapeDtypeStruct((B, T, C), BF16),
        grid=(B, T // tc),
        in_specs=[
            pl.BlockSpec((None, CONV_HALO, C), lambda b, i: (b, jnp.maximum(i * halo_blocks - 1, 0), 0)),
            pl.BlockSpec((None, tc, C), lambda b, i: (b, i, 0)),
            _const_spec((None, CONV_WIDTH, C), lambda b, i: (l, 0, 0)),
            _const_spec((None, 1, C), lambda b, i: (l, 0, 0)),
            _const_spec((None, 1, C), lambda b, i: (l, 0, 0)),
            _const_spec((None, 1, C), lambda b, i: (l, 0, 0)),
        ],
        out_specs=pl.BlockSpec((None, tc, C), lambda b, i: (b, i, 0)),
        scratch_shapes=[pltpu.VMEM((CONV_HALO + tc, C), F32)],
        compiler_params=_cparams(2),
        name="conv_module_prompt",
    )(u, u, conv_w, conv_b, ln_g, ln_b)


def _compress_halves(load_rows, w_ref, pe_ref, n):
    first = jnp.zeros((n, 2 * NSA_KV), F32)
    second = jnp.zeros((n, 2 * NSA_KV), F32)
    for l in range(CMP_STRIDE):
        xl = load_rows(l)
        a = (xl + pe_ref[l]).astype(BF16)
        b = (xl + pe_ref[CMP_STRIDE + l]).astype(BF16)
        first = first + _dot(a, w_ref[l])
        second = second + _dot(b, w_ref[CMP_STRIDE + l])
    return first, second


def _compress_prompt_body(xk_ref, xv_ref, w_ref, pe_ref, kc_ref, vc_ref, *, n):
    def load(l):
        return jnp.concatenate([r[pl.ds(l, n, stride=CMP_STRIDE), :] for r in (xk_ref, xv_ref)], axis=1)

    first, second = _compress_halves(load, w_ref, pe_ref, n)
    out = first + pltpu.roll(second, n - 1, axis=0)
    lane = lax.broadcasted_iota(jnp.int32, (n, LANES), 1)
    start = lax.broadcasted_iota(jnp.int32, (n, LANES), 0) * CMP_STRIDE
    coarse = (start - start % 4096).astype(F32)
    fine = (start % 4096).astype(F32)
    pos_fill = jnp.where(lane == AUX, fine, jnp.where(lane == AUX + 1, coarse, 0.0))
    _store_heads(kc_ref, out[:, 0:NSA_KV], [pos_fill] * NSA_KV_HEADS)
    _store_heads(vc_ref, out[:, NSA_KV:2 * NSA_KV], [jnp.ones((n, LANES), F32)] * NSA_KV_HEADS)


def _compress_prompt(nkv, w_cmp, pe_cmp, l):
    B, T, _ = nkv.shape
    n = T // CMP_STRIDE
    aug = jax.ShapeDtypeStruct((B, n, NSA_KV_HEADS * LANES), BF16)
    return pl.pallas_call(
        functools.partial(_compress_prompt_body, n=n),
        out_shape=[aug, aug],
        grid=(B,),
        in_specs=[
            pl.BlockSpec((None, T, NSA_KV), lambda b: (b, 0, 0)),
            pl.BlockSpec((None, T, NSA_KV), lambda b: (b, 0, 1)),
            _const_spec((None, CMP_LEN, 2 * NSA_KV, 2 * NSA_KV), lambda b: (l, 0, 0, 0)),
            _const_spec((None, CMP_LEN, 1, 2 * NSA_KV), lambda b: (l, 0, 0, 0)),
        ],
        out_specs=[pl.BlockSpec((None, n, NSA_KV_HEADS * LANES), lambda b: (b, 0, 0))] * 2,
        compiler_params=_cparams(1),
        name="nsa_compress_prompt",
    )(nkv, nkv, w_cmp, pe_cmp)


def _cumsum_body(sm_ref, kf_ref, tri_ref, o_ref, carry_ref):
    @pl.when(pl.program_id(1) == 0)
    def _():
        carry_ref[...] = jnp.zeros_like(carry_ref)

    tri = tri_ref[...]
    c = carry_ref[...]
    for part in _split3(sm_ref[...]):
        c = c + _dot(tri, part)
    tk = c.shape[0]
    carry_ref[...] = c[tk - 1:, :]
    lane = lax.broadcasted_iota(jnp.int32, (tk, LANES), 1)
    for h in range(FOX_HEADS):
        neg = jnp.broadcast_to(-c[:, SM_LOGF + h:SM_LOGF + h + 1], (tk, LANES))
        hi, mid, lo = (part.astype(F32) for part in _split3(neg))
        aux = jnp.where(lane == AUX, hi, jnp.where(lane == AUX + 1, mid, jnp.where(lane == AUX + 2, lo, 0.0)))
        blk = slice(h * LANES, (h + 1) * LANES)
        o_ref[:, blk] = (kf_ref[:, blk].astype(F32) + aux).astype(o_ref.dtype)


def _cumsum(sm, kf, tk):
    B, T, _ = sm.shape
    tri = jnp.asarray(np.tril(np.ones((tk, tk), np.float32)), BF16)
    return pl.pallas_call(
        _cumsum_body,
        out_shape=jax.ShapeDtypeStruct(kf.shape, BF16),
        grid=(B, T // tk),
        in_specs=[
            pl.BlockSpec((None, tk, LANES), lambda b, i: (b, i, 0)),
            pl.BlockSpec((None, tk, FOX_HEADS * LANES), lambda b, i: (b, i, 0)),
            _const_spec((tk, tk), lambda b, i: (0, 0)),
        ],
        out_specs=pl.BlockSpec((None, tk, FOX_HEADS * LANES), lambda b, i: (b, i, 0)),
        scratch_shapes=[pltpu.VMEM((1, LANES), F32)],
        compiler_params=_cparams(2),
        name="fox_cumsum",
    )(sm, kf, tri)


def _online_update(s, v, m_ref, acc_ref, idx):
    m_old = m_ref[idx]
    m_new = jnp.maximum(m_old, jnp.max(s, axis=-1, keepdims=True))
    e = jnp.exp(s - m_new)
    acc_ref[idx] = jnp.exp(m_old - m_new) * acc_ref[idx] + _dot(e.astype(BF16), v)
    m_ref[idx] = m_new


def _normalized(acc):
    return acc / acc[:, AUX:AUX + 1]


def _pair_lanes(even, odd):
    lane = lax.broadcasted_iota(jnp.int32, even.shape, 1)
    return jnp.where(lane < HEAD_DIM, even, pltpu.roll(odd, HEAD_DIM, axis=1))


def _fox_body(q_ref, k_ref, v_ref, o_ref, m_ref, acc_ref, *, tq, tk):
    t0 = pl.program_id(1) * tq
    last = (t0 + tq - 1) // tk
    qpos = t0 + lax.broadcasted_iota(jnp.int32, (tq, 1), 0)
    qs = [q_ref[:, h * LANES:(h + 1) * LANES] for h in range(FOX_HEADS)]
    m_ref[...] = jnp.full(m_ref.shape, MASKED, F32)
    acc_ref[...] = jnp.zeros(acc_ref.shape, F32)

    def tile(j, causal):
        ks = pl.multiple_of(j * tk, tk)
        for h in range(FOX_HEADS):
            blk = slice(h * LANES, (h + 1) * LANES)
            s = _dot_nt(qs[h], k_ref[pl.ds(ks, tk), blk])
            if causal:
                kpos = ks + lax.broadcasted_iota(jnp.int32, (1, tk), 1)
                s = jnp.where(kpos <= qpos, s, MASKED)
            _online_update(s, v_ref[pl.ds(ks, tk), blk], m_ref, acc_ref, h)

    def full_tile(j, carry):
        tile(j, False)
        return carry

    lax.fori_loop(0, last, full_tile, 0)
    tile(last, True)
    for p in range(FOX_HEADS // 2):
        o_ref[:, p * LANES:(p + 1) * LANES] = _pair_lanes(
            _normalized(acc_ref[2 * p]), _normalized(acc_ref[2 * p + 1])).astype(o_ref.dtype)


def _batch_spec(shape):
    return pl.BlockSpec((None,) + shape, lambda b, i: (b,) + (0,) * len(shape))


def _fox_prompt(qf, kf, vf, tq, tk):
    B, T, _ = qf.shape
    assert tk % tq == 0
    return pl.pallas_call(
        functools.partial(_fox_body, tq=tq, tk=tk),
        out_shape=jax.ShapeDtypeStruct((B, T, FOX_W), BF16),
        grid=(B, T // tq),
        in_specs=[
            pl.BlockSpec((None, tq, FOX_HEADS * LANES), lambda b, i: (b, i, 0)),
            _batch_spec((T, FOX_HEADS * LANES)),
            _batch_spec((T, FOX_HEADS * LANES)),
        ],
        out_specs=pl.BlockSpec((None, tq, FOX_W), lambda b, i: (b, i, 0)),
        scratch_shapes=[pltpu.VMEM((FOX_HEADS, tq, 1), F32), pltpu.VMEM((FOX_HEADS, tq, LANES), F32)],
        compiler_params=_cparams(2),
        name="fox_attention_prompt",
    )(qf, kf, vf)


def _masked_softmax_rows(s, valid):
    s = jnp.where(valid, s, MASKED)
    m = jnp.max(s, axis=-1, keepdims=True)
    e = jnp.where(valid, jnp.exp(s - m), 0.0)
    d = jnp.sum(e, axis=-1, keepdims=True)
    return e / jnp.where(d > 0, d, 1.0)


def _importance_matrix(nc, ns):
    ratio = SEL_LEN // CMP_STRIDE
    m = np.zeros((nc, ns), np.float32)
    for j in range(ns):
        for i in range(ratio * j, ratio * j + ratio):
            for src in (i, i - 1):
                if 0 <= src < nc:
                    m[src, j] += 1.0
    return m


def _topk_columns(score, n_pick):
    n_blk = score.shape[0]
    blk = lax.broadcasted_iota(jnp.int32, score.shape, 0).astype(F32)
    picked = jnp.zeros(score.shape, jnp.bool_)
    for _ in range(n_pick):
        m = jnp.max(score, axis=0, keepdims=True)
        first = jnp.min(jnp.where(score == m, blk, float(n_blk)), axis=0, keepdims=True)
        hit = blk == first
        picked = picked | hit
        score = jnp.where(hit, -jnp.inf, score)
    return picked


def _nsa_body(q_ref, sm_ref, kc_ref, vc_ref, ksel_ref, vsel_ref, kwin_ref, vwin_ref, ind_ref, impt_ref,
              o_ref, m_ref, acc_ref, *, tq, tk, wk):
    t0 = pl.program_id(1) * tq
    T = ksel_ref.shape[0]
    nc = kc_ref.shape[0]
    ns = impt_ref.shape[0]
    R = NSA_GROUP
    G = NSA_KV_HEADS
    last = (t0 + tq - 1) // tk
    qpos = t0 + lax.broadcasted_iota(jnp.int32, (tq, 1), 0)
    qpos_r = jnp.concatenate([qpos] * R, axis=0)
    gates = sm_ref[...]
    qa = [jnp.concatenate([q_ref[:, (g * R + r) * LANES:(g * R + r + 1) * LANES] for r in range(R)], axis=0)
          for g in range(G)]

    c_end = lax.broadcasted_iota(jnp.int32, (1, nc), 1) * CMP_STRIDE + (CMP_LEN - 1)
    valid_c = c_end <= qpos_r
    o_c, imp_t = [], []
    for g in range(G):
        blk_g = slice(g * LANES, (g + 1) * LANES)
        p_c = _masked_softmax_rows(_dot_nt(qa[g], kc_ref[:, blk_g]), valid_c)
        o_c.append(_dot(p_c.astype(BF16), vc_ref[:, blk_g]))
        imp = p_c[0:tq]
        for r in range(1, R):
            imp = imp + p_c[r * tq:(r + 1) * tq]
        acc = jnp.zeros((ns, tq), F32)
        for part in _split3(imp):
            acc = acc + _dot_nt(impt_ref[...], part)
        imp_t.append(acc)
    imp_t = jnp.concatenate(imp_t, axis=1)
    blk = lax.broadcasted_iota(jnp.int32, imp_t.shape, 0)
    cur = (t0 + lax.broadcasted_iota(jnp.int32, imp_t.shape, 1) % tq) // SEL_LEN
    forced = (blk == 0) | (blk == cur) | (blk == cur - 1)
    score = jnp.where(forced, FORCE_SCORE, imp_t)
    score = jnp.where(blk > cur, -3.0e38, score)
    picked = _topk_columns(score, min(SEL_TOPK, ns)) & (blk <= cur)
    veto = jnp.where(picked, 0.0, MASKED)

    outs = []
    for g in range(G):
        blk_g = slice(g * LANES, (g + 1) * LANES)
        veto_g = jnp.transpose(veto[:, g * tq:(g + 1) * tq]).astype(BF16)
        q_aug = jnp.concatenate([jnp.concatenate([veto_g] * R, axis=0), qa[g]], axis=1)
        m_ref[...] = jnp.full(m_ref.shape, MASKED, F32)
        acc_ref[...] = jnp.zeros(acc_ref.shape, F32)

        def tile(j, causal, blk_g=blk_g, q_aug=q_aug):
            ks = pl.multiple_of(j * tk, tk)
            k_aug = jnp.concatenate([ind_ref[pl.ds(ks, tk), :], ksel_ref[pl.ds(ks, tk), blk_g]], axis=1)
            s = _dot_nt(q_aug, k_aug)
            if causal:
                kpos = ks + lax.broadcasted_iota(jnp.int32, (1, tk), 1)
                s = jnp.where(kpos <= qpos_r, s, MASKED)
            _online_update(s, vsel_ref[pl.ds(ks, tk), blk_g], m_ref, acc_ref, 0)

        def full_tile(j, carry, tile=tile):
            tile(j, False)
            return carry

        lax.fori_loop(0, last, full_tile, 0)
        tile(last, True)
        o_s = _normalized(acc_ref[0])

        ws = pl.multiple_of(jnp.clip(t0 - WINDOW, 0, T - wk), LANES)
        dist_w = qpos_r - (ws + lax.broadcasted_iota(jnp.int32, (1, wk), 1))
        s_w = _dot_nt(qa[g], kwin_ref[pl.ds(ws, wk), blk_g])
        s_w = jnp.where((dist_w >= 0) & (dist_w <= WINDOW), s_w, MASKED)
        e_w = jnp.exp(s_w - jnp.max(s_w, axis=-1, keepdims=True))
        o_w = _normalized(_dot(e_w.astype(BF16), vwin_ref[pl.ds(ws, wk), blk_g]))

        def gate(j, g=g):
            cols = [jnp.broadcast_to(gates[:, (g * R + r) * 3 + j:(g * R + r) * 3 + j + 1], (tq, LANES))
                    for r in range(R)]
            return jnp.concatenate(cols, axis=0)

        outs.append(gate(0) * o_c[g] + gate(1) * o_s + gate(2) * o_w)
    out = _pair_lanes(outs[0], outs[1])
    for r in range(R):
        o_ref[r] = out[r * tq:(r + 1) * tq].astype(o_ref.dtype)


def _nsa_prompt(qn, sm, kc, vc, ksel, vsel, kwin, vwin, tq, tk):
    B, T, _ = qn.shape
    nc = kc.shape[1]
    ns = T // SEL_LEN
    wk = WINDOW + tq
    assert tk % tq == 0 and T >= wk
    impt = jnp.asarray(_importance_matrix(nc, ns).T, BF16)
    ind = jnp.asarray(np.arange(T)[:, None] // SEL_LEN == np.arange(ns)[None, :], BF16)
    width = NSA_KV_HEADS * LANES
    return pl.pallas_call(
        functools.partial(_nsa_body, tq=tq, tk=tk, wk=wk),
        out_shape=jax.ShapeDtypeStruct((B, NSA_GROUP, T, LANES), BF16),
        grid=(B, T // tq),
        in_specs=[
            pl.BlockSpec((None, tq, NSA_HEADS * LANES), lambda b, i: (b, i, 0)),
            pl.BlockSpec((None, tq, LANES), lambda b, i: (b, i, 0)),
            _batch_spec((nc, width)), _batch_spec((nc, width)),
            _batch_spec((T, width)), _batch_spec((T, width)),
            _batch_spec((T, width)), _batch_spec((T, width)),
            _const_spec((T, ns), lambda b, i: (0, 0)),
            _const_spec((ns, nc), lambda b, i: (0, 0)),
        ],
        out_specs=pl.BlockSpec((None, NSA_GROUP, tq, LANES), lambda b, i: (b, 0, i, 0)),
        scratch_shapes=[pltpu.VMEM((1, NSA_GROUP * tq, 1), F32), pltpu.VMEM((1, NSA_GROUP * tq, LANES), F32)],
        compiler_params=_cparams(2),
        name="nsa_attention_prompt",
    )(qn, sm, kc, vc, ksel, vsel, kwin, vwin, ind, impt)


def _out_body(x_ref, yc_ref, on_ref, of_ref, w_ref, g_ref, o_ref):
    cat = jnp.concatenate([yc_ref[...]] + [on_ref[r] for r in range(NSA_GROUP)] + [of_ref[...]], axis=1)
    y = _dot(cat, w_ref[...])
    o_ref[...] = x_ref[...] + _rms(y, g_ref[...])


def _out_proj(x, yconv, onsa, ofox, w_out_p, norm_g, l, tm):
    B, T, D = x.shape
    return pl.pallas_call(
        _out_body,
        out_shape=jax.ShapeDtypeStruct(x.shape, F32),
        grid=(B, T // tm),
        in_specs=[
            pl.BlockSpec((None, tm, D), lambda b, i: (b, i, 0)),
            pl.BlockSpec((None, tm, CONV_CH), lambda b, i: (b, i, 0)),
            pl.BlockSpec((None, NSA_GROUP, tm, LANES), lambda b, i: (b, 0, i, 0)),
            pl.BlockSpec((None, tm, FOX_W), lambda b, i: (b, i, 0)),
            _const_spec((None, D, D), lambda b, i: (l, 0, 0)),
            _const_spec((None, None, 1, D), lambda b, i: (l, 3, 0, 0)),
        ],
        out_specs=pl.BlockSpec((None, tm, D), lambda b, i: (b, i, 0)),
        compiler_params=_cparams(2),
        name="output_projection",
    )(x, yconv, onsa, ofox, w_out_p, norm_g)


def _proj_column_index():
    in_cols = 2 * CONV_CH + NSA_Q + 6 * NSA_KV + N_GATES + 3 * FOX_W + FOX_HEADS
    src_qn = 2 * CONV_CH
    src_kv = src_qn + NSA_Q
    src_gate = src_kv + 6 * NSA_KV
    src_fox = src_gate + N_GATES
    src_ff = src_fox + 3 * FOX_W
    idx = np.full((PROJ_COLS,), in_cols, np.int32)
    idx[C_GLU:C_QN] = np.arange(2 * CONV_CH)
    idx[C_QN:C_NKV] = src_qn + np.arange(NSA_Q)
    idx[C_NKV:C_QF] = src_kv + np.arange(6 * NSA_KV)
    idx[C_QF:C_FKV] = src_fox + np.arange(FOX_W)
    idx[C_FKV:C_SM] = src_fox + FOX_W + np.arange(2 * FOX_W)
    idx[C_SM:C_SM + N_GATES] = src_gate + np.arange(N_GATES)
    idx[C_SM + N_GATES:C_SM + N_GATES + FOX_HEADS] = src_ff + np.arange(FOX_HEADS)
    return idx


def _out_row_index():
    idx = np.arange(D_MODEL, dtype=np.int32)
    for r in range(NSA_GROUP):
        for g in range(NSA_KV_HEADS):
            dst = CONV_CH + r * LANES + g * HEAD_DIM
            src = CONV_CH + (g * NSA_GROUP + r) * HEAD_DIM
            idx[dst:dst + HEAD_DIM] = src + np.arange(HEAD_DIM)
    return idx


def _pack_params(w_in, w_out, cmp_pe, cmp_w, nsa_gate_b, fox_forget_b):
    depth = w_in.shape[0]
    w_ext = jnp.concatenate([w_in, jnp.zeros((depth, D_MODEL, 1), w_in.dtype)], axis=-1)
    w_proj = jnp.take(w_ext, jnp.asarray(_proj_column_index()), axis=-1).astype(BF16)
    b_small = jnp.concatenate([nsa_gate_b, fox_forget_b,
                               jnp.zeros((depth, LANES - N_GATES - FOX_HEADS), F32)], axis=-1)[:, None, :]
    w_out_p = jnp.take(w_out, jnp.asarray(_out_row_index()), axis=1).astype(BF16)
    w3 = cmp_w.reshape(depth, 2, CMP_LEN, HEAD_DIM, HEAD_DIM)
    eye_g = jnp.eye(NSA_KV_HEADS, dtype=F32)
    blocks = [jnp.einsum('ab,xlde->xladbe', eye_g, w3[:, kv]).reshape(depth, CMP_LEN, NSA_KV, NSA_KV)
              for kv in range(2)]
    zero = jnp.zeros_like(blocks[0])
    w_cmp = jnp.concatenate([jnp.concatenate([blocks[0], zero], axis=-1),
                             jnp.concatenate([zero, blocks[1]], axis=-1)], axis=-2).astype(BF16)
    pe = jnp.concatenate([cmp_pe[:, 0], cmp_pe[:, 0], cmp_pe[:, 1], cmp_pe[:, 1]], axis=-1)
    return w_proj, b_small, w_out_p, w_cmp, pe[:, :, None, :]


def _mixer_prompt(x, norm_g, packed, conv_p, l, tiles):
    w_proj, b_small, w_out_p, w_cmp, pe_cmp = packed
    conv_w, conv_b, ln_g, ln_b = conv_p
    B, T, _ = x.shape
    u, qn, nkv, ksel, vsel, win, kwin, vwin, qf, fkv, kf, vf, sm = _proj(x, norm_g, w_proj, b_small, l, tiles['tm'])
    yconv = _conv_prompt(u, conv_w, conv_b, ln_g, ln_b, l, tiles['tc'])
    kc, vc = _compress_prompt(nkv, w_cmp, pe_cmp, l)
    onsa = _nsa_prompt(qn, sm, kc, vc, ksel, vsel, kwin, vwin, tiles['nsa_tq'], tiles['nsa_tk'])
    kf = _cumsum(sm, kf, tiles['cum_tk'])
    ofox = _fox_prompt(qf, kf, vf, tiles['fox_tq'], tiles['fox_tk'])
    mixed = _out_proj(x, yconv, onsa, ofox, w_out_p, norm_g, l, tiles['tm'])
    states = (nkv.reshape(B, T, 4, NSA_KV_HEADS, HEAD_DIM),
              win[:, T - WINDOW:].reshape(B, WINDOW, 2, NSA_KV_HEADS, HEAD_DIM),
              fkv.reshape(B, T, 2, FOX_HEADS, HEAD_DIM),
              sm[:, :, SM_LOGF:SM_LOGF + FOX_HEADS],
              u[:, T - (CONV_WIDTH - 1):])
    return mixed, states


def _row_to_col(row, first_lane, stride=1):
    lane = lax.broadcasted_iota(jnp.int32, (8, LANES), 1)
    sub = lax.broadcasted_iota(jnp.int32, (8, LANES), 0)
    picked = jnp.where(lane == first_lane + stride * sub, jnp.broadcast_to(row, (8, LANES)), 0.0)
    return jnp.sum(picked, axis=-1, keepdims=True)


def _rows8(blocks):
    n = blocks[0].shape[1]
    sub = lax.broadcasted_iota(jnp.int32, (8, n), 0)
    out = jnp.zeros((8, n), F32)
    for r, blk in enumerate(blocks):
        out = jnp.where(sub == r, jnp.broadcast_to(blk, (8, n)), out)
    return out


def _conv_sample_body(st_ref, u_ref, w_ref, b_ref, lng_ref, lnb_ref, o_ref):
    acc = u_ref[...] * w_ref[CONV_WIDTH - 1:CONV_WIDTH, :]
    for k in range(CONV_WIDTH - 1):
        acc = acc + st_ref[k] * w_ref[k:k + 1, :]
    o_ref[...] = _conv_post(acc, b_ref[...], lng_ref[...], lnb_ref[...]).astype(o_ref.dtype)


def _conv_sample(state_conv_t, u, conv_w, conv_b, ln_g, ln_b, l):
    nb, C = u.shape
    hist = CONV_WIDTH - 1
    return pl.pallas_call(
        _conv_sample_body,
        out_shape=jax.ShapeDtypeStruct((nb, C), BF16),
        grid=(1,),
        in_specs=[
            pl.BlockSpec((None, hist, nb, C), lambda i: (l, 0, 0, 0)),
            pl.BlockSpec((nb, C), lambda i: (0, 0)),
            pl.BlockSpec((None, CONV_WIDTH, C), lambda i: (l, 0, 0)),
            pl.BlockSpec((None, 1, C), lambda i: (l, 0, 0)),
            pl.BlockSpec((None, 1, C), lambda i: (l, 0, 0)),
            pl.BlockSpec((None, 1, C), lambda i: (l, 0, 0)),
        ],
        out_specs=pl.BlockSpec((nb, C), lambda i: (0, 0)),
        compiler_params=_cparams(1),
        name="conv_module_sample",
    )(state_conv_t, u, conv_w, conv_b, ln_g, ln_b)


def _page_t(ref):
    v = ref[...]
    return v.reshape(v.shape[0] * v.shape[1], v.shape[2])


def _compress_sample_body(pt_ref, *refs, pg):
    pages_k = refs[:pg]
    pages_v = refs[pg:2 * pg]
    w_ref, pe_ref, o_ref, xk_ref, xv_ref = refs[2 * pg:]
    rows = PAGE_SIZE // CMP_STRIDE
    n = pg * rows
    for i in range(pg):
        xk_ref[i * PAGE_SIZE:(i + 1) * PAGE_SIZE, :] = jnp.transpose(_page_t(pages_k[i]))
        xv_ref[i * PAGE_SIZE:(i + 1) * PAGE_SIZE, :] = jnp.transpose(_page_t(pages_v[i]))

    def load(l):
        return jnp.concatenate([r[pl.ds(l, n, stride=CMP_STRIDE), :] for r in (xk_ref, xv_ref)], axis=1)

    first, second = _compress_halves(load, w_ref, pe_ref, n)
    o_ref[:, 0:2 * NSA_KV] = first
    o_ref[:, 2 * NSA_KV:4 * NSA_KV] = second


def _compress_sample(cache_nsa_t, page_table, w_cmp, pe_cmp, l, pg):
    nb, n_pages = page_table.shape
    rows = PAGE_SIZE // CMP_STRIDE
    page_specs = [
        pl.BlockSpec((None, None, None, NSA_KV_HEADS, HEAD_DIM, PAGE_SIZE),
                     lambda b, j, pt, i=i, c=c: (l, pt[b, j * pg + i], c, 0, 0, 0))
        for c in range(2) for i in range(pg)]
    return pl.pallas_call(
        functools.partial(_compress_sample_body, pg=pg),
        out_shape=jax.ShapeDtypeStruct((nb, n_pages * rows, 4 * NSA_KV), F32),
        grid_spec=pltpu.PrefetchScalarGridSpec(
            num_scalar_prefetch=1,
            grid=(nb, n_pages // pg),
            in_specs=page_specs + [
                _const_spec((None, CMP_LEN, 2 * NSA_KV, 2 * NSA_KV), lambda b, j, pt: (l, 0, 0, 0)),
                _const_spec((None, CMP_LEN, 1, 2 * NSA_KV), lambda b, j, pt: (l, 0, 0, 0)),
            ],
            out_specs=pl.BlockSpec((None, pg * rows, 4 * NSA_KV), lambda b, j, pt: (b, j, 0)),
            scratch_shapes=[pltpu.VMEM((pg * PAGE_SIZE, NSA_KV), F32), pltpu.VMEM((pg * PAGE_SIZE, NSA_KV), F32)],
        ),
        compiler_params=_cparams(2),
        name="nsa_compress_sample",
    )(page_table, *([cache_nsa_t] * (2 * pg)), w_cmp, pe_cmp)


def _sample_q(q_ref, g):
    rows = [q_ref[:, (g * NSA_GROUP + r) * LANES:(g * NSA_GROUP + r + 1) * LANES].astype(F32)
            for r in range(NSA_GROUP)]
    tile = _rows8(rows)
    lane = lax.broadcasted_iota(jnp.int32, tile.shape, 1)
    tile = jnp.where(lane < HEAD_DIM, tile, 0.0)
    return pltpu.roll(tile, HEAD_DIM, axis=1) if g else tile


def _bf16_round(x):
    return x.astype(BF16).astype(F32)


def _sample_slopes(g):
    sub = lax.broadcasted_iota(jnp.int32, (8, 1), 0)
    slope = jnp.zeros((8, 1), F32)
    for r in range(NSA_GROUP):
        slope = jnp.where(sub == r, 2.0 ** (-8.0 * (g * NSA_GROUP + r + 1) / NSA_HEADS), slope)
    return slope


def _softmax_with_new(s_past, valid, s_new):
    s_past = jnp.where(valid, s_past, MASKED)
    m = jnp.maximum(jnp.max(s_past, axis=-1, keepdims=True), s_new)
    e = jnp.where(valid, jnp.exp(s_past - m), 0.0)
    e_new = jnp.exp(s_new - m)
    d = jnp.sum(e, axis=-1, keepdims=True) + e_new
    return e / d, e_new / d


def _nsa_sample_a_body(q_ref, sm_ref, fs_ref, kw_ref, vw_ref, winn_ref, impm_ref, part_ref, gsel_ref, pick_ref,
                       *, past, ns):
    nc = fs_ref.shape[0]
    wb = kw_ref.shape[2]
    nsp = impm_ref.shape[1]
    lane = lax.broadcasted_iota(jnp.int32, (8, LANES), 1)
    sub = lax.broadcasted_iota(jnp.int32, (8, LANES), 0)
    kcvc = (fs_ref[:, 0:2 * NSA_KV] + pltpu.roll(fs_ref[:, 2 * NSA_KV:4 * NSA_KV], nc - 1, axis=0)).astype(BF16)
    kw_t = _page_t(kw_ref).astype(BF16)
    vw_t = _page_t(vw_ref).astype(BF16)
    kw_new = _bf16_round(winn_ref[:, 0:NSA_KV])
    vw_new = _bf16_round(winn_ref[:, NSA_KV:2 * NSA_KV])
    sm_row = sm_ref[...]
    part = None
    gsel = None
    picks = jnp.zeros((8, LANES), jnp.int32)
    for g in range(NSA_KV_HEADS):
        q32 = _sample_q(q_ref, g)
        q = q32.astype(BF16)
        slope = _sample_slopes(g)
        c_end = lax.broadcasted_iota(jnp.int32, (1, nc), 1) * CMP_STRIDE + (CMP_LEN - 1)
        dist_c = past - c_end
        s_c = _dot_nt(q, kcvc[:, 0:LANES]) - slope * dist_c.astype(F32)
        p_c = _masked_softmax_rows(s_c, jnp.broadcast_to(dist_c >= 0, s_c.shape))
        o_c = _dot(p_c.astype(BF16), kcvc[:, LANES:2 * LANES])
        sub_c = lax.broadcasted_iota(jnp.int32, p_c.shape, 0)
        imp = jnp.sum(jnp.where(sub_c < NSA_GROUP, p_c, 0.0), axis=0, keepdims=True)
        imp = jnp.broadcast_to(imp, p_c.shape)
        imp_sel = jnp.zeros((8, nsp), F32)
        for piece in _split3(imp):
            imp_sel = imp_sel + _dot(piece, impm_ref[...])
        blk = lax.broadcasted_iota(jnp.int32, (8, nsp), 1)
        cur = past // SEL_LEN
        forced = (blk == 0) | (blk == cur) | (blk == cur - 1)
        score = jnp.where(forced, FORCE_SCORE, imp_sel)
        score = jnp.where((blk > cur) | (blk >= ns), -3.0e38, score)
        blk_f = blk.astype(F32)
        for it in range(min(SEL_TOPK, ns)):
            m = jnp.max(score, axis=-1, keepdims=True)
            first = jnp.min(jnp.where(score == m, blk_f, float(nsp)), axis=-1, keepdims=True)
            score = jnp.where(blk_f == first, -jnp.inf, score)
            picks = jnp.where((lane == it) & (sub == g), first.astype(jnp.int32), picks)
        dist_w = wb - lax.broadcasted_iota(jnp.int32, (1, wb), 1)
        s_w = _dot(q, kw_t) - slope * dist_w.astype(F32)
        s_new = jnp.sum(q32 * kw_new, axis=-1, keepdims=True)
        p_w, p_new = _softmax_with_new(s_w, jnp.broadcast_to(dist_w <= WINDOW, s_w.shape), s_new)
        o_w = _dot_nt(p_w.astype(BF16), vw_t) + _bf16_round(p_new) * vw_new
        g_c = _row_to_col(sm_row, g * NSA_GROUP * 3 + 0, 3)
        g_s = _row_to_col(sm_row, g * NSA_GROUP * 3 + 1, 3)
        g_w = _row_to_col(sm_row, g * NSA_GROUP * 3 + 2, 3)
        pg_ = g_c * o_c + g_w * o_w
        gs_ = jnp.broadcast_to(g_s, (8, LANES))
        part = pg_ if part is None else jnp.where(lane < g * HEAD_DIM, part, pg_)
        gsel = gs_ if gsel is None else jnp.where(lane < g * HEAD_DIM, gsel, gs_)
    part_ref[...] = part
    gsel_ref[...] = gsel
    pick_ref[...] = picks


def _nsa_sample_a(qn, sm, fs, state_win_t, win_new, l, past):
    nb = qn.shape[0]
    nc = fs.shape[1]
    wb = state_win_t.shape[5]
    win_spec = lambda kv: pl.BlockSpec((None, None, None, NSA_KV_HEADS, HEAD_DIM, wb),
                                       lambda b: (l, b, kv, 0, 0, 0))
    ns = -(-(past + 1) // SEL_LEN)
    nsp = -(-ns // LANES) * LANES
    impm = jnp.asarray(_importance_matrix(nc, nsp), BF16)
    row = lambda w: pl.BlockSpec((None, 1, w), lambda b: (b, 0, 0))
    tile = pl.BlockSpec((None, 8, LANES), lambda b: (b, 0, 0))
    return pl.pallas_call(
        functools.partial(_nsa_sample_a_body, past=past, ns=ns),
        out_shape=[jax.ShapeDtypeStruct((nb, 8, LANES), F32), jax.ShapeDtypeStruct((nb, 8, LANES), F32),
                   jax.ShapeDtypeStruct((nb, 8, LANES), jnp.int32)],
        grid=(nb,),
        in_specs=[
            row(NSA_HEADS * LANES), row(LANES),
            pl.BlockSpec((None, nc, 4 * NSA_KV), lambda b: (b, 0, 0)),
            win_spec(0), win_spec(1),
            row(2 * NSA_KV),
            _const_spec((nc, nsp), lambda b: (0, 0)),
        ],
        out_specs=[tile, tile, tile],
        compiler_params=_cparams(1),
        name="nsa_sample_compressed_window",
    )(qn, sm, fs, state_win_t, state_win_t, win_new, impm)


def _nsa_sample_b_body(pt_ref, pick_ref, *refs, past, n_pick, pg):
    k_pages = refs[:pg]
    v_pages = refs[pg:2 * pg]
    q_ref, new_ref, part_ref, gsel_ref, o_ref, m_ref, l_ref, acc_ref = refs[2 * pg:]
    b = pl.program_id(0)
    j = pl.program_id(1)
    lane = lax.broadcasted_iota(jnp.int32, (8, LANES), 1)
    kidx = lax.broadcasted_iota(jnp.int32, (1, pg * PAGE_SIZE), 1)
    kpos = j * (pg * PAGE_SIZE) + kidx
    blk = kpos // SEL_LEN
    qs = [_sample_q(q_ref, g) for g in range(NSA_KV_HEADS)]

    @pl.when(j == 0)
    def _():
        ks_new = _bf16_round(new_ref[:, 2 * NSA_KV:3 * NSA_KV])
        vs_new = _bf16_round(new_ref[:, 3 * NSA_KV:4 * NSA_KV])
        for g in range(NSA_KV_HEADS):
            m_ref[g] = jnp.sum(qs[g] * ks_new, axis=-1, keepdims=True)
            l_ref[g] = jnp.ones((8, 1), F32)
            acc_ref[g] = jnp.broadcast_to(vs_new, (8, LANES))

    kt = [_page_t(p).astype(BF16) for p in k_pages]
    vt = [_page_t(p).astype(BF16) for p in v_pages]
    for g in range(NSA_KV_HEADS):
        q = qs[g].astype(BF16)
        live = jnp.zeros(kidx.shape, jnp.int32)
        for i in range(n_pick):
            live = jnp.where(blk == pick_ref[b, g * n_pick + i], 1, live)
        valid = jnp.broadcast_to(live > 0, (8, pg * PAGE_SIZE))
        s = jnp.concatenate([_dot(q, kt[i]) for i in range(pg)], axis=1)
        s = jnp.where(valid, s - _sample_slopes(g) * (past - kpos).astype(F32), MASKED)
        m_old = m_ref[g]
        m_new = jnp.maximum(m_old, jnp.max(s, axis=-1, keepdims=True))
        e = jnp.where(valid, jnp.exp(s - m_new), 0.0)
        alpha = jnp.exp(m_old - m_new)
        l_ref[g] = alpha * l_ref[g] + jnp.sum(e, axis=-1, keepdims=True)
        acc = alpha * acc_ref[g]
        e = e.astype(BF16)
        for i in range(pg):
            acc = acc + _dot_nt(e[:, i * PAGE_SIZE:(i + 1) * PAGE_SIZE], vt[i])
        acc_ref[g] = acc
        m_ref[g] = m_new

    @pl.when(j == pl.num_programs(1) - 1)
    def _():
        o_s = jnp.where(lane < HEAD_DIM, acc_ref[0] / l_ref[0], acc_ref[1] / l_ref[1])
        o_ref[...] = (part_ref[...] + gsel_ref[...] * o_s).astype(o_ref.dtype)


def _nsa_sample_b(cache_nsa_t, page_table, picks, qn, nkv_new, part, gsel, l, past, n_pick, pg):
    nb, n_pages = page_table.shape
    assert past % SEL_LEN == 0
    page_specs = [pl.BlockSpec((None, None, None, NSA_KV_HEADS, HEAD_DIM, PAGE_SIZE),
                               lambda b, j, pt, pk, i=i, c=c: (l, pt[b, j * pg + i], c, 0, 0, 0))
                  for c in (2, 3) for i in range(pg)]
    row = lambda w: pl.BlockSpec((None, 1, w), lambda b, j, pt, pk: (b, 0, 0))
    tile = pl.BlockSpec((None, 8, LANES), lambda b, j, pt, pk: (b, 0, 0))
    return pl.pallas_call(
        functools.partial(_nsa_sample_b_body, past=past, n_pick=n_pick, pg=pg),
        out_shape=jax.ShapeDtypeStruct((nb, 8, LANES), BF16),
        grid_spec=pltpu.PrefetchScalarGridSpec(
            num_scalar_prefetch=2,
            grid=(nb, n_pages // pg),
            in_specs=page_specs + [row(NSA_HEADS * LANES), row(4 * NSA_KV), tile, tile],
            out_specs=tile,
            scratch_shapes=[pltpu.VMEM((NSA_KV_HEADS, 8, 1), F32), pltpu.VMEM((NSA_KV_HEADS, 8, 1), F32),
                            pltpu.VMEM((NSA_KV_HEADS, 8, LANES), F32)],
        ),
        compiler_params=_cparams(2),
        name="nsa_sample_selected",
    )(page_table, picks, *([cache_nsa_t] * (2 * pg)), qn, nkv_new, part, gsel)


def _fox_sample_body(pt_ref, *refs, pg):
    pages_k = refs[:pg]
    pages_v = refs[pg:2 * pg]
    lfs = refs[2 * pg:3 * pg]
    q_ref, new_ref, sm_ref, ust_ref, aft_ref, o_ref, m_ref, l_ref, acc_ref, carry_ref = refs[3 * pg:]
    jj = pl.program_id(1)
    lane2 = lax.broadcasted_iota(jnp.int32, (8, FOX_W), 1)
    sub2 = lax.broadcasted_iota(jnp.int32, (8, FOX_W), 0)
    lane1 = lax.broadcasted_iota(jnp.int32, (8, LANES), 1)
    sub1 = lax.broadcasted_iota(jnp.int32, (8, LANES), 0)
    heads = [q_ref[:, h * LANES:(h + 1) * LANES].astype(F32) for h in range(FOX_HEADS)]
    zero = jnp.zeros((1, LANES), F32)
    even = jnp.where(lane1 < HEAD_DIM, _rows8([heads[h] if h % 2 == 0 else zero for h in range(FOX_HEADS)]), 0.0)
    odd = jnp.where(lane1 < HEAD_DIM, _rows8([heads[h] if h % 2 else zero for h in range(FOX_HEADS)]), 0.0)
    pair = even + pltpu.roll(odd, HEAD_DIM, axis=1)
    q32 = jnp.concatenate([jnp.where(sub1 // 2 == p, pair, 0.0) for p in range(FOX_HEADS // 2)], axis=1)
    q = q32.astype(BF16)

    @pl.when(jj == 0)
    def _():
        s_new = jnp.sum(q32 * _bf16_round(new_ref[:, 0:FOX_W]), axis=-1, keepdims=True)
        m_ref[...] = s_new
        l_ref[...] = jnp.ones(l_ref.shape, F32)
        acc_ref[...] = jnp.broadcast_to(new_ref[:, FOX_W:2 * FOX_W].astype(BF16).astype(F32), (8, FOX_W))
        carry_ref[...] = _row_to_col(sm_ref[...], SM_LOGF)

    lf = jnp.concatenate([r[...] for r in lfs], axis=0)
    ones = jnp.ones((LANES, LANES), BF16)
    within = jnp.zeros(lf.shape, F32)
    total = jnp.zeros(lf.shape, F32)
    for piece in _split3(lf):
        within = within + _dot(piece, ust_ref[...])
        total = total + _dot(piece, ones)
    after = jnp.zeros(lf.shape, F32)
    for piece in _split3(total):
        after = after + _dot(aft_ref[...], piece)
    carry = carry_ref[...]
    bias = within + after + jnp.concatenate([carry] * pg, axis=0)
    s = jnp.concatenate([_dot(q, _page_t(p).astype(BF16)) for p in pages_k], axis=0) + bias
    s3 = s.reshape(pg, 8, LANES)
    m_old = m_ref[...]
    m_new = jnp.maximum(m_old, jnp.max(jnp.max(s3, axis=0), axis=-1, keepdims=True))
    e3 = jnp.exp(s3 - m_new[None])
    alpha = jnp.exp(m_old - m_new)
    l_ref[...] = alpha * l_ref[...] + jnp.sum(jnp.sum(e3, axis=0), axis=-1, keepdims=True)
    acc = alpha * acc_ref[...]
    for i in range(pg):
        acc = acc + _dot_nt(e3[i].astype(BF16), _page_t(pages_v[i]).astype(BF16))
    acc_ref[...] = acc
    m_ref[...] = m_new
    carry_ref[...] = bias[0:8, 0:1] + lf[0:8, 0:1]

    @pl.when(jj == pl.num_programs(1) - 1)
    def _():
        o = acc_ref[...] / l_ref[...]
        o_ref[...] = jnp.sum(jnp.where(lane2 // HEAD_DIM == sub2, o, 0.0), axis=0, keepdims=True).astype(o_ref.dtype)


def _fox_sample(cache_fox_t, lft, page_table, qf, fkv_new, sm, l, pg):
    nb, n_pages = page_table.shape
    steps = n_pages // pg
    ust = jnp.asarray(np.triu(np.ones((LANES, LANES), np.float32), 1).T, BF16)
    idx = np.arange(pg * 8)
    aft = ((idx[:, None] % 8 == idx[None, :] % 8) & (idx[None, :] // 8 > idx[:, None] // 8)).astype(np.float32)

    def page_of(b, j, pt, i):
        return pt[b, (steps - 1 - j) * pg + i]

    page_specs = [pl.BlockSpec((None, None, None, FOX_HEADS, HEAD_DIM, PAGE_SIZE),
                               lambda b, j, pt, i=i, kv=kv: (l, page_of(b, j, pt, i), kv, 0, 0, 0))
                  for kv in range(2) for i in range(pg)]
    lf_specs = [pl.BlockSpec((None, None, 8, PAGE_SIZE),
                             lambda b, j, pt, i=i: (l, page_of(b, j, pt, i), 0, 0)) for i in range(pg)]
    row = lambda w: pl.BlockSpec((None, 1, w), lambda b, j, pt: (b, 0, 0))
    return pl.pallas_call(
        functools.partial(_fox_sample_body, pg=pg),
        out_shape=jax.ShapeDtypeStruct((nb, 1, FOX_W), BF16),
        grid_spec=pltpu.PrefetchScalarGridSpec(
            num_scalar_prefetch=1,
            grid=(nb, steps),
            in_specs=page_specs + lf_specs + [
                row(FOX_HEADS * LANES), row(2 * FOX_W), row(LANES),
                _const_spec((LANES, LANES), lambda b, j, pt: (0, 0)),
                _const_spec((pg * 8, pg * 8), lambda b, j, pt: (0, 0)),
            ],
            out_specs=row(FOX_W),
            scratch_shapes=[pltpu.VMEM((8, 1), F32), pltpu.VMEM((8, 1), F32),
                            pltpu.VMEM((8, FOX_W), F32), pltpu.VMEM((8, 1), F32)],
        ),
        compiler_params=_cparams(2),
        name="fox_attention_sample",
    )(page_table, *([cache_fox_t] * (2 * pg)), *([lft] * pg), qf, fkv_new, sm, ust, jnp.asarray(aft, BF16))


def _mixer_sample(x, norm_g, packed, conv_p, caches, page_table, l, tiles):
    w_proj, b_small, w_out_p, w_cmp, pe_cmp = packed
    conv_w, conv_b, ln_g, ln_b = conv_p
    cache_nsa_t, state_win, state_win_t, cache_fox_t, lft, state_conv, state_conv_t = caches
    nb = x.shape[1]
    past = page_table.shape[1] * PAGE_SIZE
    u, qn, nkv, _, _, win, _, _, qf, fkv, _, _, sm = _proj(x, norm_g, w_proj, b_small, l, nb)
    per_seq = lambda a: a.reshape(nb, 1, a.shape[-1])
    yconv = _conv_sample(state_conv_t, u[0], conv_w, conv_b, ln_g, ln_b, l)
    fs = _compress_sample(cache_nsa_t, page_table, w_cmp, pe_cmp, l, tiles['cmp_pg'])
    part, gsel, picks = _nsa_sample_a(per_seq(qn), per_seq(sm), fs, state_win_t, per_seq(win), l, past)
    n_pick = min(SEL_TOPK, -(-(past + 1) // SEL_LEN))
    picks2 = picks[:, :NSA_KV_HEADS, :n_pick].reshape(nb, NSA_KV_HEADS * n_pick)
    onsa = _nsa_sample_b(cache_nsa_t, page_table, picks2, per_seq(qn), per_seq(nkv), part, gsel, l, past, n_pick,
                         tiles['cmp_pg'])
    onsa = jnp.transpose(onsa[:, :NSA_GROUP], (1, 0, 2))[None]
    ofox = _fox_sample(cache_fox_t, lft, page_table, per_seq(qf), per_seq(fkv), per_seq(sm), l, tiles['fox_pg'])
    mixed = _out_proj(x, yconv[None], onsa, ofox.reshape(1, nb, FOX_W), w_out_p, norm_g, l, nb)
    wb = state_win.shape[2]
    win_state = jnp.concatenate([state_win[l][:, 1:], win[0].reshape(nb, 1, 2, NSA_KV_HEADS, HEAD_DIM)], axis=1)
    conv_state = jnp.concatenate([state_conv[l][:, 1:], u[0][:, None, :]], axis=1)
    states = (nkv.reshape(nb, 1, 4, NSA_KV_HEADS, HEAD_DIM),
              win_state,
              fkv.reshape(nb, 1, 2, FOX_HEADS, HEAD_DIM),
              sm[0][:, None, SM_LOGF:SM_LOGF + FOX_HEADS],
              conv_state)
    return mixed, states


def _tiles(T, n_pages):
    return dict(tm=min(512, T), tc=min(512, T), nsa_tq=256, nsa_tk=min(1024, T), fox_tq=min(256, T),
                fox_tk=min(2048, T), cum_tk=min(512, T), cmp_pg=min(16, n_pages), fox_pg=min(16, n_pages))


def kernel(x_prompt, x_sample, cache_nsa_kv, state_nsa_win, cache_fox_kv, cache_fox_logf, state_conv,
           page_table, norm_g, ffn_w_gate, ffn_w_up, ffn_w_down, w_in, w_out, conv_w, conv_b,
           conv_ln_g, conv_ln_b, cmp_pe, cmp_w, nsa_gate_b, fox_forget_b):
    depth = w_in.shape[0]
    B, T, D = x_prompt.shape
    nb, ts, _ = x_sample.shape
    n_phys = cache_nsa_kv.shape[1]
    n_pages = page_table.shape[1]
    wb = state_nsa_win.shape[2]
    assert ts == 1 and D == D_MODEL and T >= WINDOW + 128 and wb == min(WINDOW, n_pages * PAGE_SIZE)
    tiles = _tiles(T, n_pages)
    assert n_pages % tiles['cmp_pg'] == 0 and n_pages % tiles['fox_pg'] == 0
    assert PAGE_SIZE % SEL_LEN == 0 and T % tiles['tm'] == 0 and T % tiles['fox_tk'] == 0 and T % tiles['nsa_tk'] == 0

    packed = _pack_params(w_in, w_out, cmp_pe, cmp_w, nsa_gate_b, fox_forget_b)
    wg, wu, wd = (w.astype(BF16) for w in (ffn_w_gate, ffn_w_up, ffn_w_down))
    ng = norm_g[:, :, None, :]
    conv_p = (conv_w, conv_b[:, None, :], conv_ln_g[:, None, :], conv_ln_b[:, None, :])
    rows_last = (0, 1, 3, 4, 5, 2)
    cache_nsa_t = jnp.transpose(cache_nsa_kv, rows_last)
    cache_fox_t = jnp.transpose(cache_fox_kv, rows_last)
    state_win_t = jnp.transpose(state_nsa_win, rows_last)
    state_conv_t = jnp.transpose(state_conv, (0, 2, 1, 3))
    lft = jnp.pad(jnp.swapaxes(cache_fox_logf, 2, 3), ((0, 0), (0, 0), (0, 8 - FOX_HEADS), (0, 0)))
    caches = (cache_nsa_t, state_nsa_win, state_win_t, cache_fox_t, lft, state_conv, state_conv_t)

    xp = x_prompt
    xs = x_sample.reshape(1, nb, D)
    p_states, s_states = [], []
    for l in range(depth):
        xp = _ffn(xp, ng, wg, wu, wd, l, 0, tiles['tm'])
        xs = _ffn(xs, ng, wg, wu, wd, l, 0, nb)
        xp, ps = _mixer_prompt(xp, ng, packed, conv_p, l, tiles)
        xs, ss = _mixer_sample(xs, ng, packed, conv_p, caches, page_table, l, tiles)
        xp = _ffn(xp, ng, wg, wu, wd, l, 1, tiles['tm'])
        xs = _ffn(xs, ng, wg, wu, wd, l, 1, nb)
        p_states.append(ps)
        s_states.append(ss)
    stack = lambda states, k: jnp.stack([s[k] for s in states])
    return (xp, xs.reshape(nb, 1, D),
            *(stack(p_states, k) for k in range(5)),
            *(stack(s_states, k) for k in range(5)))
```

```python
import functools

import jax
import jax.numpy as jnp
import numpy as np
from jax import lax
from jax.experimental import pallas as pl
from jax.experimental.pallas import tpu as pltpu

F32 = jnp.float32
BF16 = jnp.bfloat16

D_MODEL = 1024
D_FF = 2816
HEAD_DIM = 64
CONV_CH = 256
CONV_WIDTH = 31
NSA_HEADS = 8
NSA_KV_HEADS = 2
NSA_GROUP = NSA_HEADS // NSA_KV_HEADS
FOX_HEADS = 4
CMP_LEN = 32
CMP_STRIDE = 16
SEL_LEN = 64
SEL_TOPK = 16
WINDOW = 512
PAGE_SIZE = 128
RMS_EPS = 1e-6
LN_EPS = 1e-5
FORCE_SCORE = 1e9

NSA_Q = NSA_HEADS * HEAD_DIM
NSA_KV = NSA_KV_HEADS * HEAD_DIM
FOX_W = FOX_HEADS * HEAD_DIM
N_GATES = 3 * NSA_HEADS

LANES = 128
V7X_VMEM_BYTES = 64 * 1024 * 1024
VMEM_LIMIT = (V7X_VMEM_BYTES * 7) // 8

MASKED = -1e30
FF_CHUNK = 256
CONV_HALO = 32

C_GLU = 0
C_QN = C_GLU + 2 * CONV_CH
C_NKV = C_QN + NSA_Q
C_WIN = C_NKV + 4 * NSA_KV
C_QF = C_WIN + 2 * NSA_KV
C_FKV = C_QF + FOX_W
C_SM = C_FKV + 2 * FOX_W
PROJ_COLS = C_SM + LANES
SM_LOGF = N_GATES

AUX = HEAD_DIM

NT_DIMS = (((1,), (1,)), ((), ()))


def _cparams(n_grid):
    return pltpu.CompilerParams(dimension_semantics=("arbitrary",) * n_grid,
                                vmem_limit_bytes=VMEM_LIMIT)


def _const_spec(shape, index):
    return pl.BlockSpec(shape, index, pipeline_mode=pl.Buffered(1))


def _rms(x, g):
    return x * lax.rsqrt(jnp.mean(x * x, axis=-1, keepdims=True) + RMS_EPS) * g


def _split3(x):
    hi = x.astype(BF16)
    r1 = x - hi.astype(F32)
    mid = r1.astype(BF16)
    lo = (r1 - mid.astype(F32)).astype(BF16)
    return hi, mid, lo


def _dot(a, b):
    return jnp.dot(a, b, preferred_element_type=F32)


def _dot_nt(a, b):
    return lax.dot_general(a, b, NT_DIMS, preferred_element_type=F32)


def _ffn_math(x, gpre, gpost, wg_ref, wu_ref, wd_ref):
    xn = _rms(x, gpre).astype(BF16)
    acc = jnp.zeros(x.shape, F32)
    for c in range(D_FF // FF_CHUNK):
        sl = slice(c * FF_CHUNK, (c + 1) * FF_CHUNK)
        g = _dot(xn, wg_ref[:, sl])
        u = _dot(xn, wu_ref[:, sl])
        h = (g * jax.nn.sigmoid(g) * u).astype(BF16)
        acc = acc + _dot(h, wd_ref[sl, :])
    return x + 0.5 * _rms(acc, gpost)


def _ffn_body(x_ref, gpre_ref, gpost_ref, wg_ref, wu_ref, wd_ref, o_ref):
    o_ref[...] = _ffn_math(x_ref[...], gpre_ref[...], gpost_ref[...], wg_ref, wu_ref, wd_ref)


def _ffn(x, norm_g, wg, wu, wd, l, k, tm):
    B, T, D = x.shape
    gi = 4 * k
    return pl.pallas_call(
        _ffn_body,
        out_shape=jax.ShapeDtypeStruct(x.shape, F32),
        grid=(B, T // tm),
        in_specs=[
            pl.BlockSpec((None, tm, D), lambda b, i: (b, i, 0)),
            _const_spec((None, None, 1, D), lambda b, i: (l, gi, 0, 0)),
            _const_spec((None, None, 1, D), lambda b, i: (l, gi + 1, 0, 0)),
            _const_spec((None, None, D, D_FF), lambda b, i: (l, k, 0, 0)),
            _const_spec((None, None, D, D_FF), lambda b, i: (l, k, 0, 0)),
            _const_spec((None, None, D_FF, D), lambda b, i: (l, k, 0, 0)),
        ],
        out_specs=pl.BlockSpec((None, tm, D), lambda b, i: (b, i, 0)),
        compiler_params=_cparams(2),
        name="ffn_half_step",
    )(x, norm_g, norm_g, wg, wu, wd)


def _alibi_slope(head):
    return 2.0 ** (-8.0 * (head + 1) / NSA_HEADS)


def _store_heads(o_ref, z, fills):
    lane = lax.broadcasted_iota(jnp.int32, (z.shape[0], LANES), 1)
    for h, fill in enumerate(fills):
        blk = z[:, (h // 2) * LANES:(h // 2 + 1) * LANES]
        if h % 2:
            blk = pltpu.roll(blk, HEAD_DIM, axis=1)
        o_ref[:, h * LANES:(h + 1) * LANES] = jnp.where(lane < HEAD_DIM, blk, fill).astype(o_ref.dtype)


def _position_fill(pos, lane):
    within = (pos % SEL_LEN).astype(F32)
    block = (pos - pos % SEL_LEN).astype(F32)
    return jnp.where(lane == AUX, within, jnp.where(lane == AUX + 1, block, 0.0))


def _proj_body(x_ref, g_ref, w_ref, b_ref, u_ref, qn_ref, nkv_ref, ksel_ref, vsel_ref, win_ref, kwin_ref,
               vwin_ref, qf_ref, fkv_ref, kf_ref, vf_ref, sm_ref):
    xn = _rms(x_ref[...], g_ref[...]).astype(BF16)
    tm = xn.shape[0]
    lane = lax.broadcasted_iota(jnp.int32, (tm, LANES), 1)
    pos = pl.program_id(1) * tm + lax.broadcasted_iota(jnp.int32, (tm, LANES), 0)
    pos_fill = _position_fill(pos, lane)
    ones_fill = jnp.ones((tm, LANES), F32)
    zero_fill = jnp.zeros((tm, LANES), F32)

    def mm(lo, hi):
        return _dot(xn, w_ref[:, lo:hi])

    z = mm(C_GLU, C_QN)
    u_ref[...] = z[:, :CONV_CH] * jax.nn.sigmoid(z[:, CONV_CH:])
    slope_fills = [jnp.where((lane == AUX) | (lane == AUX + 1), _alibi_slope(h), 0.0) for h in range(NSA_HEADS)]
    _store_heads(qn_ref, mm(C_QN, C_NKV) * (HEAD_DIM ** -0.5), slope_fills)
    z = mm(C_NKV, C_WIN)
    nkv_ref[...] = z
    _store_heads(ksel_ref, z[:, 2 * NSA_KV:3 * NSA_KV], [pos_fill] * NSA_KV_HEADS)
    _store_heads(vsel_ref, z[:, 3 * NSA_KV:4 * NSA_KV], [ones_fill] * NSA_KV_HEADS)
    z = mm(C_WIN, C_QF)
    win_ref[...] = z
    _store_heads(kwin_ref, z[:, 0:NSA_KV], [pos_fill] * NSA_KV_HEADS)
    _store_heads(vwin_ref, z[:, NSA_KV:2 * NSA_KV], [ones_fill] * NSA_KV_HEADS)
    one3 = jnp.where((lane >= AUX) & (lane < AUX + 3), 1.0, 0.0)
    _store_heads(qf_ref, mm(C_QF, C_FKV) * (HEAD_DIM ** -0.5), [one3] * FOX_HEADS)
    z = mm(C_FKV, C_SM)
    fkv_ref[...] = z
    _store_heads(kf_ref, z[:, 0:FOX_W], [zero_fill] * FOX_HEADS)
    _store_heads(vf_ref, z[:, FOX_W:2 * FOX_W], [ones_fill] * FOX_HEADS)
    z = mm(C_SM, PROJ_COLS) + b_ref[...]
    log_sig = jnp.minimum(z, 0.0) - jnp.log1p(jnp.exp(-jnp.abs(z)))
    sm_ref[...] = jnp.where(lane < N_GATES, jax.nn.sigmoid(z), log_sig)


def _proj(x, norm_g, w_proj, b_small, l, tm):
    B, T, D = x.shape
    widths = [(CONV_CH, F32), (NSA_HEADS * LANES, BF16), (4 * NSA_KV, F32), (NSA_KV_HEADS * LANES, BF16),
              (NSA_KV_HEADS * LANES, BF16), (2 * NSA_KV, F32), (NSA_KV_HEADS * LANES, BF16),
              (NSA_KV_HEADS * LANES, BF16), (FOX_HEADS * LANES, BF16), (2 * FOX_W, F32),
              (FOX_HEADS * LANES, BF16), (FOX_HEADS * LANES, BF16), (LANES, F32)]
    return pl.pallas_call(
        _proj_body,
        out_shape=[jax.ShapeDtypeStruct((B, T, w), dt) for w, dt in widths],
        grid=(B, T // tm),
        in_specs=[
            pl.BlockSpec((None, tm, D), lambda b, i: (b, i, 0)),
            _const_spec((None, None, 1, D), lambda b, i: (l, 2, 0, 0)),
            _const_spec((None, D, PROJ_COLS), lambda b, i: (l, 0, 0)),
            _const_spec((None, 1, LANES), lambda b, i: (l, 0, 0)),
        ],
        out_specs=[pl.BlockSpec((None, tm, w), lambda b, i: (b, i, 0)) for w, _ in widths],
        compiler_params=_cparams(2),
        name="input_projection",
    )(x, norm_g, w_proj, b_small)


def _conv_body(prev_ref, cur_ref, w_ref, b_ref, lng_ref, lnb_ref, o_ref, hist_ref, *, tc):
    i = pl.program_id(1)
    prev = prev_ref[...]
    hist_ref[0:CONV_HALO, :] = jnp.where(i == 0, 0.0, prev)
    hist_ref[CONV_HALO:CONV_HALO + tc, :] = cur_ref[...]
    off = CONV_HALO - (CONV_WIDTH - 1)
    acc = jnp.zeros((tc, CONV_CH), F32)
    for k in range(CONV_WIDTH):
        acc = acc + hist_ref[off + k:off + k + tc, :] * w_ref[k:k + 1, :]
    o_ref[...] = _conv_post(acc, b_ref[...], lng_ref[...], lnb_ref[...]).astype(o_ref.dtype)


def _conv_post(acc, b, ln_g, ln_b):
    y = acc + b
    mu = jnp.mean(y, axis=-1, keepdims=True)
    var = jnp.mean(jnp.square(y - mu), axis=-1, keepdims=True)
    y = (y - mu) * lax.rsqrt(var + LN_EPS) * ln_g + ln_b
    return y * jax.nn.sigmoid(y)


def _conv_prompt(u, conv_w, conv_b, ln_g, ln_b, l, tc):
    B, T, C = u.shape
    halo_blocks = tc // CONV_HALO
    return pl.pallas_call(
        functools.partial(_conv_body, tc=tc),
        out_shape=jax.ShapeDtypeStruct((B, T, C), BF16),
        grid=(B, T // tc),
        in_specs=[
            pl.BlockSpec((None, CONV_HALO, C), lambda b, i: (b, jnp.maximum(i * halo_blocks - 1, 0), 0)),
            pl.BlockSpec((None, tc, C), lambda b, i: (b, i, 0)),
            _const_spec((None, CONV_WIDTH, C), lambda b, i: (l, 0, 0)),
            _const_spec((None, 1, C), lambda b, i: (l, 0, 0)),
            _const_spec((None, 1, C), lambda b, i: (l, 0, 0)),
            _const_spec((None, 1, C), lambda b, i: (l, 0, 0)),
        ],
        out_specs=pl.BlockSpec((None, tc, C), lambda b, i: (b, i, 0)),
        scratch_shapes=[pltpu.VMEM((CONV_HALO + tc, C), F32)],
        compiler_params=_cparams(2),
        name="conv_module_prompt",
    )(u, u, conv_w, conv_b, ln_g, ln_b)


def _compress_halves(load_rows, w_ref, pe_ref, n):
    first = jnp.zeros((n, 2 * NSA_KV), F32)
    second = jnp.zeros((n, 2 * NSA_KV), F32)
    for l in range(CMP_STRIDE):
        xl = load_rows(l)
        a = (xl + pe_ref[l]).astype(BF16)
        b = (xl + pe_ref[CMP_STRIDE + l]).astype(BF16)
        first = first + _dot(a, w_ref[l])
        second = second + _dot(b, w_ref[CMP_STRIDE + l])
    return first, second


def _compress_prompt_body(xk_ref, xv_ref, w_ref, pe_ref, kc_ref, vc_ref, *, n):
    def load(l):
        return jnp.concatenate([r[pl.ds(l, n, stride=CMP_STRIDE), :] for r in (xk_ref, xv_ref)], axis=1)

    first, second = _compress_halves(load, w_ref, pe_ref, n)
    out = first + pltpu.roll(second, n - 1, axis=0)
    lane = lax.broadcasted_iota(jnp.int32, (n, LANES), 1)
    start = lax.broadcasted_iota(jnp.int32, (n, LANES), 0) * CMP_STRIDE
    coarse = (start - start % 4096).astype(F32)
    fine = (start % 4096).astype(F32)
    pos_fill = jnp.where(lane == AUX, fine, jnp.where(lane == AUX + 1, coarse, 0.0))
    _store_heads(kc_ref, out[:, 0:NSA_KV], [pos_fill] * NSA_KV_HEADS)
    _store_heads(vc_ref, out[:, NSA_KV:2 * NSA_KV], [jnp.ones((n, LANES), F32)] * NSA_KV_HEADS)


def _compress_prompt(nkv, w_cmp, pe_cmp, l):
    B, T, _ = nkv.shape
    n = T // CMP_STRIDE
    aug = jax.ShapeDtypeStruct((B, n, NSA_KV_HEADS * LANES), BF16)
    return pl.pallas_call(
        functools.partial(_compress_prompt_body, n=n),
        out_shape=[aug, aug],
        grid=(B,),
        in_specs=[
            pl.BlockSpec((None, T, NSA_KV), lambda b: (b, 0, 0)),
            pl.BlockSpec((None, T, NSA_KV), lambda b: (b, 0, 1)),
            _const_spec((None, CMP_LEN, 2 * NSA_KV, 2 * NSA_KV), lambda b: (l, 0, 0, 0)),
            _const_spec((None, CMP_LEN, 1, 2 * NSA_KV), lambda b: (l, 0, 0, 0)),
        ],
        out_specs=[pl.BlockSpec((None, n, NSA_KV_HEADS * LANES), lambda b: (b, 0, 0))] * 2,
        compiler_params=_cparams(1),
        name="nsa_compress_prompt",
    )(nkv, nkv, w_cmp, pe_cmp)


def _cumsum_body(sm_ref, kf_ref, tri_ref, o_ref, carry_ref):
    @pl.when(pl.program_id(1) == 0)
    def _():
        carry_ref[...] = jnp.zeros_like(carry_ref)

    tri = tri_ref[...]
    c = carry_ref[...]
    for part in _split3(sm_ref[...]):
        c = c + _dot(tri, part)
    tk = c.shape[0]
    carry_ref[...] = c[tk - 1:, :]
    lane = lax.broadcasted_iota(jnp.int32, (tk, LANES), 1)
    for h in range(FOX_HEADS):
        neg = jnp.broadcast_to(-c[:, SM_LOGF + h:SM_LOGF + h + 1], (tk, LANES))
        hi, mid, lo = (part.astype(F32) for part in _split3(neg))
        aux = jnp.where(lane == AUX, hi, jnp.where(lane == AUX + 1, mid, jnp.where(lane == AUX + 2, lo, 0.0)))
        blk = slice(h * LANES, (h + 1) * LANES)
        o_ref[:, blk] = (kf_ref[:, blk].astype(F32) + aux).astype(o_ref.dtype)


def _cumsum(sm, kf, tk):
    B, T, _ = sm.shape
    tri = jnp.asarray(np.tril(np.ones((tk, tk), np.float32)), BF16)
    return pl.pallas_call(
        _cumsum_body,
        out_shape=jax.ShapeDtypeStruct(kf.shape, BF16),
        grid=(B, T // tk),
        in_specs=[
            pl.BlockSpec((None, tk, LANES), lambda b, i: (b, i, 0)),
            pl.BlockSpec((None, tk, FOX_HEADS * LANES), lambda b, i: (b, i, 0)),
            _const_spec((tk, tk), lambda b, i: (0, 0)),
        ],
        out_specs=pl.BlockSpec((None, tk, FOX_HEADS * LANES), lambda b, i: (b, i, 0)),
        scratch_shapes=[pltpu.VMEM((1, LANES), F32)],
        compiler_params=_cparams(2),
        name="fox_cumsum",
    )(sm, kf, tri)


def _online_update(s, v, m_ref, acc_ref, idx):
    m_old = m_ref[idx]
    m_new = jnp.maximum(m_old, jnp.max(s, axis=-1, keepdims=True))
    e = jnp.exp(s - m_new)
    acc_ref[idx] = jnp.exp(m_old - m_new) * acc_ref[idx] + _dot(e.astype(BF16), v)
    m_ref[idx] = m_new


def _normalized(acc):
    return acc / acc[:, AUX:AUX + 1]


def _pair_lanes(even, odd):
    lane = lax.broadcasted_iota(jnp.int32, even.shape, 1)
    return jnp.where(lane < HEAD_DIM, even, pltpu.roll(odd, HEAD_DIM, axis=1))


def _fox_body(q_ref, k_ref, v_ref, o_ref, m_ref, acc_ref, *, tq, tk):
    t0 = pl.program_id(1) * tq
    last = (t0 + tq - 1) // tk
    qpos = t0 + lax.broadcasted_iota(jnp.int32, (tq, 1), 0)
    qs = [q_ref[:, h * LANES:(h + 1) * LANES] for h in range(FOX_HEADS)]
    m_ref[...] = jnp.full(m_ref.shape, MASKED, F32)
    acc_ref[...] = jnp.zeros(acc_ref.shape, F32)

    def tile(j, causal):
        ks = pl.multiple_of(j * tk, tk)
        for h in range(FOX_HEADS):
            blk = slice(h * LANES, (h + 1) * LANES)
            s = _dot_nt(qs[h], k_ref[pl.ds(ks, tk), blk])
            if causal:
                kpos = ks + lax.broadcasted_iota(jnp.int32, (1, tk), 1)
                s = jnp.where(kpos <= qpos, s, MASKED)
            _online_update(s, v_ref[pl.ds(ks, tk), blk], m_ref, acc_ref, h)

    def full_tile(j, carry):
        tile(j, False)
        return carry

    lax.fori_loop(0, last, full_tile, 0)
    tile(last, True)
    for p in range(FOX_HEADS // 2):
        o_ref[:, p * LANES:(p + 1) * LANES] = _pair_lanes(
            _normalized(acc_ref[2 * p]), _normalized(acc_ref[2 * p + 1])).astype(o_ref.dtype)


def _batch_spec(shape):
    return pl.BlockSpec((None,) + shape, lambda b, i: (b,) + (0,) * len(shape), pipeline_mode=pl.Buffered(1))


def _fox_prompt(qf, kf, vf, tq, tk):
    B, T, _ = qf.shape
    assert tk % tq == 0
    return pl.pallas_call(
        functools.partial(_fox_body, tq=tq, tk=tk),
        out_shape=jax.ShapeDtypeStruct((B, T, FOX_W), BF16),
        grid=(B, T // tq),
        in_specs=[
            pl.BlockSpec((None, tq, FOX_HEADS * LANES), lambda b, i: (b, i, 0)),
            _batch_spec((T, FOX_HEADS * LANES)),
            _batch_spec((T, FOX_HEADS * LANES)),
        ],
        out_specs=pl.BlockSpec((None, tq, FOX_W), lambda b, i: (b, i, 0)),
        scratch_shapes=[pltpu.VMEM((FOX_HEADS, tq, 1), F32), pltpu.VMEM((FOX_HEADS, tq, LANES), F32)],
        compiler_params=_cparams(2),
        name="fox_attention_prompt",
    )(qf, kf, vf)


def _masked_softmax_rows(s, valid):
    s = jnp.where(valid, s, MASKED)
    m = jnp.max(s, axis=-1, keepdims=True)
    e = jnp.where(valid, jnp.exp(s - m), 0.0)
    d = jnp.sum(e, axis=-1, keepdims=True)
    return e / jnp.where(d > 0, d, 1.0)


def _importance_matrix(nc, ns):
    ratio = SEL_LEN // CMP_STRIDE
    m = np.zeros((nc, ns), np.float32)
    for j in range(ns):
        for i in range(ratio * j, ratio * j + ratio):
            for src in (i, i - 1):
                if 0 <= src < nc:
                    m[src, j] += 1.0
    return m


def _topk_columns(score, n_pick):
    n_blk = score.shape[0]
    blk = lax.broadcasted_iota(jnp.int32, score.shape, 0).astype(F32)
    picked = jnp.zeros(score.shape, jnp.bool_)
    for _ in range(n_pick):
        m = jnp.max(score, axis=0, keepdims=True)
        first = jnp.min(jnp.where(score == m, blk, float(n_blk)), axis=0, keepdims=True)
        hit = blk == first
        picked = picked | hit
        score = jnp.where(hit, -jnp.inf, score)
    return picked


def _nsa_body(q_ref, sm_ref, kc_ref, vc_ref, ksel_ref, vsel_ref, kwin_ref, vwin_ref, ind_ref, impt_ref,
              o_ref, m_ref, acc_ref, *, tq, tk, wk):
    t0 = pl.program_id(1) * tq
    T = ksel_ref.shape[0]
    nc = kc_ref.shape[0]
    ns = impt_ref.shape[0]
    R = NSA_GROUP
    G = NSA_KV_HEADS
    last = (t0 + tq - 1) // tk
    qpos = t0 + lax.broadcasted_iota(jnp.int32, (tq, 1), 0)
    qpos_r = jnp.concatenate([qpos] * R, axis=0)
    gates = sm_ref[...]
    qa = [jnp.concatenate([q_ref[:, (g * R + r) * LANES:(g * R + r + 1) * LANES] for r in range(R)], axis=0)
          for g in range(G)]

    c_end = lax.broadcasted_iota(jnp.int32, (1, nc), 1) * CMP_STRIDE + (CMP_LEN - 1)
    valid_c = c_end <= qpos_r
    o_c, imp_t = [], []
    for g in range(G):
        blk_g = slice(g * LANES, (g + 1) * LANES)
        p_c = _masked_softmax_rows(_dot_nt(qa[g], kc_ref[:, blk_g]), valid_c)
        o_c.append(_dot(p_c.astype(BF16), vc_ref[:, blk_g]))
        imp = p_c[0:tq]
        for r in range(1, R):
            imp = imp + p_c[r * tq:(r + 1) * tq]
        acc = jnp.zeros((ns, tq), F32)
        for part in _split3(imp):
            acc = acc + _dot_nt(impt_ref[...], part)
        imp_t.append(acc)
    imp_t = jnp.concatenate(imp_t, axis=1)
    blk = lax.broadcasted_iota(jnp.int32, imp_t.shape, 0)
    cur = (t0 + lax.broadcasted_iota(jnp.int32, imp_t.shape, 1) % tq) // SEL_LEN
    forced = (blk == 0) | (blk == cur) | (blk == cur - 1)
    score = jnp.where(forced, FORCE_SCORE, imp_t)
    score = jnp.where(blk > cur, -3.0e38, score)
    picked = _topk_columns(score, min(SEL_TOPK, ns)) & (blk <= cur)
    veto = jnp.where(picked, 0.0, MASKED)

    outs = []
    for g in range(G):
        blk_g = slice(g * LANES, (g + 1) * LANES)
        veto_g = jnp.transpose(veto[:, g * tq:(g + 1) * tq]).astype(BF16)
        q_aug = jnp.concatenate([jnp.concatenate([veto_g] * R, axis=0), qa[g]], axis=1)
        m_ref[...] = jnp.full(m_ref.shape, MASKED, F32)
        acc_ref[...] = jnp.zeros(acc_ref.shape, F32)

        def tile(j, causal, blk_g=blk_g, q_aug=q_aug):
            ks = pl.multiple_of(j * tk, tk)
            k_aug = jnp.concatenate([ind_ref[pl.ds(ks, tk), :], ksel_ref[pl.ds(ks, tk), blk_g]], axis=1)
            s = _dot_nt(q_aug, k_aug)
            if causal:
                kpos = ks + lax.broadcasted_iota(jnp.int32, (1, tk), 1)
                s = jnp.where(kpos <= qpos_r, s, MASKED)
            _online_update(s, vsel_ref[pl.ds(ks, tk), blk_g], m_ref, acc_ref, 0)

        def full_tile(j, carry, tile=tile):
            tile(j, False)
            return carry

        lax.fori_loop(0, last, full_tile, 0)
        tile(last, True)
        o_s = _normalized(acc_ref[0])

        ws = pl.multiple_of(jnp.clip(t0 - WINDOW, 0, T - wk), LANES)
        dist_w = qpos_r - (ws + lax.broadcasted_iota(jnp.int32, (1, wk), 1))
        s_w = _dot_nt(qa[g], kwin_ref[pl.ds(ws, wk), blk_g])
        s_w = jnp.where((dist_w >= 0) & (dist_w <= WINDOW), s_w, MASKED)
        e_w = jnp.exp(s_w - jnp.max(s_w, axis=-1, keepdims=True))
        o_w = _normalized(_dot(e_w.astype(BF16), vwin_ref[pl.ds(ws, wk), blk_g]))

        def gate(j, g=g):
            cols = [jnp.broadcast_to(gates[:, (g * R + r) * 3 + j:(g * R + r) * 3 + j + 1], (tq, LANES))
                    for r in range(R)]
            return jnp.concatenate(cols, axis=0)

        outs.append(gate(0) * o_c[g] + gate(1) * o_s + gate(2) * o_w)
    out = _pair_lanes(outs[0], outs[1])
    for r in range(R):
        o_ref[r] = out[r * tq:(r + 1) * tq].astype(o_ref.dtype)


def _nsa_prompt(qn, sm, kc, vc, ksel, vsel, kwin, vwin, tq, tk):
    B, T, _ = qn.shape
    nc = kc.shape[1]
    ns = T // SEL_LEN
    wk = WINDOW + tq
    assert tk % tq == 0 and T >= wk
    impt = jnp.asarray(_importance_matrix(nc, ns).T, BF16)
    ind = jnp.asarray(np.arange(T)[:, None] // SEL_LEN == np.arange(ns)[None, :], BF16)
    width = NSA_KV_HEADS * LANES
    return pl.pallas_call(
        functools.partial(_nsa_body, tq=tq, tk=tk, wk=wk),
        out_shape=jax.ShapeDtypeStruct((B, NSA_GROUP, T, LANES), BF16),
        grid=(B, T // tq),
        in_specs=[
            pl.BlockSpec((None, tq, NSA_HEADS * LANES), lambda b, i: (b, i, 0)),
            pl.BlockSpec((None, tq, LANES), lambda b, i: (b, i, 0)),
            _batch_spec((nc, width)), _batch_spec((nc, width)),
            _batch_spec((T, width)), _batch_spec((T, width)),
            _batch_spec((T, width)), _batch_spec((T, width)),
            _const_spec((T, ns), lambda b, i: (0, 0)),
            _const_spec((ns, nc), lambda b, i: (0, 0)),
        ],
        out_specs=pl.BlockSpec((None, NSA_GROUP, tq, LANES), lambda b, i: (b, 0, i, 0)),
        scratch_shapes=[pltpu.VMEM((1, NSA_GROUP * tq, 1), F32), pltpu.VMEM((1, NSA_GROUP * tq, LANES), F32)],
        compiler_params=_cparams(2),
        name="nsa_attention_prompt",
    )(qn, sm, kc, vc, ksel, vsel, kwin, vwin, ind, impt)


def _out_body(x_ref, yc_ref, on_ref, of_ref, w_ref, g_ref, o_ref):
    cat = jnp.concatenate([yc_ref[...]] + [on_ref[r] for r in range(NSA_GROUP)] + [of_ref[...]], axis=1)
    y = _dot(cat, w_ref[...])
    o_ref[...] = x_ref[...] + _rms(y, g_ref[...])


def _out_proj(x, yconv, onsa, ofox, w_out_p, norm_g, l, tm):
    B, T, D = x.shape
    return pl.pallas_call(
        _out_body,
        out_shape=jax.ShapeDtypeStruct(x.shape, F32),
        grid=(B, T // tm),
        in_specs=[
            pl.BlockSpec((None, tm, D), lambda b, i: (b, i, 0)),
            pl.BlockSpec((None, tm, CONV_CH), lambda b, i: (b, i, 0)),
            pl.BlockSpec((None, NSA_GROUP, tm, LANES), lambda b, i: (b, 0, i, 0)),
            pl.BlockSpec((None, tm, FOX_W), lambda b, i: (b, i, 0)),
            _const_spec((None, D, D), lambda b, i: (l, 0, 0)),
            _const_spec((None, None, 1, D), lambda b, i: (l, 3, 0, 0)),
        ],
        out_specs=pl.BlockSpec((None, tm, D), lambda b, i: (b, i, 0)),
        compiler_params=_cparams(2),
        name="output_projection",
    )(x, yconv, onsa, ofox, w_out_p, norm_g)


def _proj_column_index():
    in_cols = 2 * CONV_CH + NSA_Q + 6 * NSA_KV + N_GATES + 3 * FOX_W + FOX_HEADS
    src_qn = 2 * CONV_CH
    src_kv = src_qn + NSA_Q
    src_gate = src_kv + 6 * NSA_KV
    src_fox = src_gate + N_GATES
    src_ff = src_fox + 3 * FOX_W
    idx = np.full((PROJ_COLS,), in_cols, np.int32)
    idx[C_GLU:C_QN] = np.arange(2 * CONV_CH)
    idx[C_QN:C_NKV] = src_qn + np.arange(NSA_Q)
    idx[C_NKV:C_QF] = src_kv + np.arange(6 * NSA_KV)
    idx[C_QF:C_FKV] = src_fox + np.arange(FOX_W)
    idx[C_FKV:C_SM] = src_fox + FOX_W + np.arange(2 * FOX_W)
    idx[C_SM:C_SM + N_GATES] = src_gate + np.arange(N_GATES)
    idx[C_SM + N_GATES:C_SM + N_GATES + FOX_HEADS] = src_ff + np.arange(FOX_HEADS)
    return idx


def _out_row_index():
    idx = np.arange(D_MODEL, dtype=np.int32)
    for r in range(NSA_GROUP):
        for g in range(NSA_KV_HEADS):
            dst = CONV_CH + r * LANES + g * HEAD_DIM
            src = CONV_CH + (g * NSA_GROUP + r) * HEAD_DIM
            idx[dst:dst + HEAD_DIM] = src + np.arange(HEAD_DIM)
    return idx


def _pack_params(w_in, w_out, cmp_pe, cmp_w, nsa_gate_b, fox_forget_b):
    depth = w_in.shape[0]
    w_ext = jnp.concatenate([w_in, jnp.zeros((depth, D_MODEL, 1), w_in.dtype)], axis=-1)
    w_proj = jnp.take(w_ext, jnp.asarray(_proj_column_index()), axis=-1).astype(BF16)
    b_small = jnp.concatenate([nsa_gate_b, fox_forget_b,
                               jnp.zeros((depth, LANES - N_GATES - FOX_HEADS), F32)], axis=-1)[:, None, :]
    w_out_p = jnp.take(w_out, jnp.asarray(_out_row_index()), axis=1).astype(BF16)
    w3 = cmp_w.reshape(depth, 2, CMP_LEN, HEAD_DIM, HEAD_DIM)
    eye_g = jnp.eye(NSA_KV_HEADS, dtype=F32)
    blocks = [jnp.einsum('ab,xlde->xladbe', eye_g, w3[:, kv]).reshape(depth, CMP_LEN, NSA_KV, NSA_KV)
              for kv in range(2)]
    zero = jnp.zeros_like(blocks[0])
    w_cmp = jnp.concatenate([jnp.concatenate([blocks[0], zero], axis=-1),
                             jnp.concatenate([zero, blocks[1]], axis=-1)], axis=-2).astype(BF16)
    pe = jnp.concatenate([cmp_pe[:, 0], cmp_pe[:, 0], cmp_pe[:, 1], cmp_pe[:, 1]], axis=-1)
    return w_proj, b_small, w_out_p, w_cmp, pe[:, :, None, :]


def _mixer_prompt(x, norm_g, packed, conv_p, l, tiles):
    w_proj, b_small, w_out_p, w_cmp, pe_cmp = packed
    conv_w, conv_b, ln_g, ln_b = conv_p
    B, T, _ = x.shape
    u, qn, nkv, ksel, vsel, win, kwin, vwin, qf, fkv, kf, vf, sm = _proj(x, norm_g, w_proj, b_small, l, tiles['tm'])
    yconv = _conv_prompt(u, conv_w, conv_b, ln_g, ln_b, l, tiles['tc'])
    kc, vc = _compress_prompt(nkv, w_cmp, pe_cmp, l)
    onsa = _nsa_prompt(qn, sm, kc, vc, ksel, vsel, kwin, vwin, tiles['nsa_tq'], tiles['nsa_tk'])
    kf = _cumsum(sm, kf, tiles['cum_tk'])
    ofox = _fox_prompt(qf, kf, vf, tiles['fox_tq'], tiles['fox_tk'])
    mixed = _out_proj(x, yconv, onsa, ofox, w_out_p, norm_g, l, tiles['tm'])
    states = (nkv.reshape(B, T, 4, NSA_KV_HEADS, HEAD_DIM),
              win[:, T - WINDOW:].reshape(B, WINDOW, 2, NSA_KV_HEADS, HEAD_DIM),
              fkv.reshape(B, T, 2, FOX_HEADS, HEAD_DIM),
              sm[:, :, SM_LOGF:SM_LOGF + FOX_HEADS],
              u[:, T - (CONV_WIDTH - 1):])
    return mixed, states


def _row_to_col(row, first_lane, stride=1):
    lane = lax.broadcasted_iota(jnp.int32, (8, LANES), 1)
    sub = lax.broadcasted_iota(jnp.int32, (8, LANES), 0)
    picked = jnp.where(lane == first_lane + stride * sub, jnp.broadcast_to(row, (8, LANES)), 0.0)
    return jnp.sum(picked, axis=-1, keepdims=True)


def _rows8(blocks):
    n = blocks[0].shape[1]
    sub = lax.broadcasted_iota(jnp.int32, (8, n), 0)
    out = jnp.zeros((8, n), F32)
    for r, blk in enumerate(blocks):
        out = jnp.where(sub == r, jnp.broadcast_to(blk, (8, n)), out)
    return out


def _conv_sample_body(st_ref, u_ref, w_ref, b_ref, lng_ref, lnb_ref, o_ref):
    acc = u_ref[...] * w_ref[CONV_WIDTH - 1:CONV_WIDTH, :]
    for k in range(CONV_WIDTH - 1):
        acc = acc + st_ref[k] * w_ref[k:k + 1, :]
    o_ref[...] = _conv_post(acc, b_ref[...], lng_ref[...], lnb_ref[...]).astype(o_ref.dtype)


def _conv_sample(state_conv_t, u, conv_w, conv_b, ln_g, ln_b, l):
    nb, C = u.shape
    hist = CONV_WIDTH - 1
    return pl.pallas_call(
        _conv_sample_body,
        out_shape=jax.ShapeDtypeStruct((nb, C), BF16),
        grid=(1,),
        in_specs=[
            pl.BlockSpec((None, hist, nb, C), lambda i: (l, 0, 0, 0)),
            pl.BlockSpec((nb, C), lambda i: (0, 0)),
            pl.BlockSpec((None, CONV_WIDTH, C), lambda i: (l, 0, 0)),
            pl.BlockSpec((None, 1, C), lambda i: (l, 0, 0)),
            pl.BlockSpec((None, 1, C), lambda i: (l, 0, 0)),
            pl.BlockSpec((None, 1, C), lambda i: (l, 0, 0)),
        ],
        out_specs=pl.BlockSpec((nb, C), lambda i: (0, 0)),
        compiler_params=_cparams(1),
        name="conv_module_sample",
    )(state_conv_t, u, conv_w, conv_b, ln_g, ln_b)


def _page_t(ref):
    v = ref[...]
    return v.reshape(v.shape[0] * v.shape[1], v.shape[2])


def _compress_sample_body(pt_ref, *refs, pg):
    pages_k = refs[:pg]
    pages_v = refs[pg:2 * pg]
    w_ref, pe_ref, o_ref, xk_ref, xv_ref = refs[2 * pg:]
    rows = PAGE_SIZE // CMP_STRIDE
    n = pg * rows
    for i in range(pg):
        xk_ref[i * PAGE_SIZE:(i + 1) * PAGE_SIZE, :] = jnp.transpose(_page_t(pages_k[i]))
        xv_ref[i * PAGE_SIZE:(i + 1) * PAGE_SIZE, :] = jnp.transpose(_page_t(pages_v[i]))

    def load(l):
        return jnp.concatenate([r[pl.ds(l, n, stride=CMP_STRIDE), :] for r in (xk_ref, xv_ref)], axis=1)

    first, second = _compress_halves(load, w_ref, pe_ref, n)
    o_ref[:, 0:2 * NSA_KV] = first
    o_ref[:, 2 * NSA_KV:4 * NSA_KV] = second


def _compress_sample(cache_nsa_t, page_table, w_cmp, pe_cmp, l, pg):
    nb, n_pages = page_table.shape
    rows = PAGE_SIZE // CMP_STRIDE
    page_specs = [
        pl.BlockSpec((None, None, None, NSA_KV_HEADS, HEAD_DIM, PAGE_SIZE),
                     lambda b, j, pt, i=i, c=c: (l, pt[b, j * pg + i], c, 0, 0, 0))
        for c in range(2) for i in range(pg)]
    return pl.pallas_call(
        functools.partial(_compress_sample_body, pg=pg),
        out_shape=jax.ShapeDtypeStruct((nb, n_pages * rows, 4 * NSA_KV), F32),
        grid_spec=pltpu.PrefetchScalarGridSpec(
            num_scalar_prefetch=1,
            grid=(nb, n_pages // pg),
            in_specs=page_specs + [
                _const_spec((None, CMP_LEN, 2 * NSA_KV, 2 * NSA_KV), lambda b, j, pt: (l, 0, 0, 0)),
                _const_spec((None, CMP_LEN, 1, 2 * NSA_KV), lambda b, j, pt: (l, 0, 0, 0)),
            ],
            out_specs=pl.BlockSpec((None, pg * rows, 4 * NSA_KV), lambda b, j, pt: (b, j, 0)),
            scratch_shapes=[pltpu.VMEM((pg * PAGE_SIZE, NSA_KV), F32), pltpu.VMEM((pg * PAGE_SIZE, NSA_KV), F32)],
        ),
        compiler_params=_cparams(2),
        name="nsa_compress_sample",
    )(page_table, *([cache_nsa_t] * (2 * pg)), w_cmp, pe_cmp)


def _sample_q(q_ref, g):
    rows = [q_ref[:, (g * NSA_GROUP + r) * LANES:(g * NSA_GROUP + r + 1) * LANES].astype(F32)
            for r in range(NSA_GROUP)]
    tile = _rows8(rows)
    lane = lax.broadcasted_iota(jnp.int32, tile.shape, 1)
    tile = jnp.where(lane < HEAD_DIM, tile, 0.0)
    return pltpu.roll(tile, HEAD_DIM, axis=1) if g else tile


def _bf16_round(x):
    return x.astype(BF16).astype(F32)


def _sample_slopes(g):
    sub = lax.broadcasted_iota(jnp.int32, (8, 1), 0)
    slope = jnp.zeros((8, 1), F32)
    for r in range(NSA_GROUP):
        slope = jnp.where(sub == r, 2.0 ** (-8.0 * (g * NSA_GROUP + r + 1) / NSA_HEADS), slope)
    return slope


def _softmax_with_new(s_past, valid, s_new):
    s_past = jnp.where(valid, s_past, MASKED)
    m = jnp.maximum(jnp.max(s_past, axis=-1, keepdims=True), s_new)
    e = jnp.where(valid, jnp.exp(s_past - m), 0.0)
    e_new = jnp.exp(s_new - m)
    d = jnp.sum(e, axis=-1, keepdims=True) + e_new
    return e / d, e_new / d


def _nsa_sample_a_body(q_ref, sm_ref, fs_ref, kw_ref, vw_ref, winn_ref, impm_ref, part_ref, gsel_ref, pick_ref,
                       *, past, ns):
    nc = fs_ref.shape[0]
    wb = kw_ref.shape[2]
    nsp = impm_ref.shape[1]
    lane = lax.broadcasted_iota(jnp.int32, (8, LANES), 1)
    sub = lax.broadcasted_iota(jnp.int32, (8, LANES), 0)
    kcvc = (fs_ref[:, 0:2 * NSA_KV] + pltpu.roll(fs_ref[:, 2 * NSA_KV:4 * NSA_KV], nc - 1, axis=0)).astype(BF16)
    kw_t = _page_t(kw_ref).astype(BF16)
    vw_t = _page_t(vw_ref).astype(BF16)
    kw_new = _bf16_round(winn_ref[:, 0:NSA_KV])
    vw_new = _bf16_round(winn_ref[:, NSA_KV:2 * NSA_KV])
    sm_row = sm_ref[...]
    part = None
    gsel = None
    picks = jnp.zeros((8, LANES), jnp.int32)
    for g in range(NSA_KV_HEADS):
        q32 = _sample_q(q_ref, g)
        q = q32.astype(BF16)
        slope = _sample_slopes(g)
        c_end = lax.broadcasted_iota(jnp.int32, (1, nc), 1) * CMP_STRIDE + (CMP_LEN - 1)
        dist_c = past - c_end
        s_c = _dot_nt(q, kcvc[:, 0:LANES]) - slope * dist_c.astype(F32)
        p_c = _masked_softmax_rows(s_c, jnp.broadcast_to(dist_c >= 0, s_c.shape))
        o_c = _dot(p_c.astype(BF16), kcvc[:, LANES:2 * LANES])
        sub_c = lax.broadcasted_iota(jnp.int32, p_c.shape, 0)
        imp = jnp.sum(jnp.where(sub_c < NSA_GROUP, p_c, 0.0), axis=0, keepdims=True)
        imp = jnp.broadcast_to(imp, p_c.shape)
        imp_sel = jnp.zeros((8, nsp), F32)
        for piece in _split3(imp):
            imp_sel = imp_sel + _dot(piece, impm_ref[...])
        blk = lax.broadcasted_iota(jnp.int32, (8, nsp), 1)
        cur = past // SEL_LEN
        forced = (blk == 0) | (blk == cur) | (blk == cur - 1)
        score = jnp.where(forced, FORCE_SCORE, imp_sel)
        score = jnp.where((blk > cur) | (blk >= ns), -3.0e38, score)
        blk_f = blk.astype(F32)
        for it in range(min(SEL_TOPK, ns)):
            m = jnp.max(score, axis=-1, keepdims=True)
            first = jnp.min(jnp.where(score == m, blk_f, float(nsp)), axis=-1, keepdims=True)
            score = jnp.where(blk_f == first, -jnp.inf, score)
            picks = jnp.where((lane == it) & (sub == g), first.astype(jnp.int32), picks)
        dist_w = wb - lax.broadcasted_iota(jnp.int32, (1, wb), 1)
        s_w = _dot(q, kw_t) - slope * dist_w.astype(F32)
        s_new = jnp.sum(q32 * kw_new, axis=-1, keepdims=True)
        p_w, p_new = _softmax_with_new(s_w, jnp.broadcast_to(dist_w <= WINDOW, s_w.shape), s_new)
        o_w = _dot_nt(p_w.astype(BF16), vw_t) + _bf16_round(p_new) * vw_new
        g_c = _row_to_col(sm_row, g * NSA_GROUP * 3 + 0, 3)
        g_s = _row_to_col(sm_row, g * NSA_GROUP * 3 + 1, 3)
        g_w = _row_to_col(sm_row, g * NSA_GROUP * 3 + 2, 3)
        pg_ = g_c * o_c + g_w * o_w
        gs_ = jnp.broadcast_to(g_s, (8, LANES))
        part = pg_ if part is None else jnp.where(lane < g * HEAD_DIM, part, pg_)
        gsel = gs_ if gsel is None else jnp.where(lane < g * HEAD_DIM, gsel, gs_)
    part_ref[...] = part
    gsel_ref[...] = gsel
    pick_ref[...] = picks


def _nsa_sample_a(qn, sm, fs, state_win_t, win_new, l, past):
    nb = qn.shape[0]
    nc = fs.shape[1]
    wb = state_win_t.shape[5]
    win_spec = lambda kv: pl.BlockSpec((None, None, None, NSA_KV_HEADS, HEAD_DIM, wb),
                                       lambda b: (l, b, kv, 0, 0, 0))
    ns = -(-(past + 1) // SEL_LEN)
    nsp = -(-ns // LANES) * LANES
    impm = jnp.asarray(_importance_matrix(nc, nsp), BF16)
    row = lambda w: pl.BlockSpec((None, 1, w), lambda b: (b, 0, 0))
    tile = pl.BlockSpec((None, 8, LANES), lambda b: (b, 0, 0))
    return pl.pallas_call(
        functools.partial(_nsa_sample_a_body, past=past, ns=ns),
        out_shape=[jax.ShapeDtypeStruct((nb, 8, LANES), F32), jax.ShapeDtypeStruct((nb, 8, LANES), F32),
                   jax.ShapeDtypeStruct((nb, 8, LANES), jnp.int32)],
        grid=(nb,),
        in_specs=[
            row(NSA_HEADS * LANES), row(LANES),
            pl.BlockSpec((None, nc, 4 * NSA_KV), lambda b: (b, 0, 0)),
            win_spec(0), win_spec(1),
            row(2 * NSA_KV),
            _const_spec((nc, nsp), lambda b: (0, 0)),
        ],
        out_specs=[tile, tile, tile],
        compiler_params=_cparams(1),
        name="nsa_sample_compressed_window",
    )(qn, sm, fs, state_win_t, state_win_t, win_new, impm)


def _nsa_sample_b_body(pt_ref, pick_ref, *refs, past, n_pick, pg):
    k_pages = refs[:pg]
    v_pages = refs[pg:2 * pg]
    q_ref, new_ref, part_ref, gsel_ref, o_ref, m_ref, l_ref, acc_ref = refs[2 * pg:]
    b = pl.program_id(0)
    j = pl.program_id(1)
    lane = lax.broadcasted_iota(jnp.int32, (8, LANES), 1)
    kidx = lax.broadcasted_iota(jnp.int32, (1, pg * PAGE_SIZE), 1)
    kpos = j * (pg * PAGE_SIZE) + kidx
    blk = kpos // SEL_LEN
    qs = [_sample_q(q_ref, g) for g in range(NSA_KV_HEADS)]

    @pl.when(j == 0)
    def _():
        ks_new = _bf16_round(new_ref[:, 2 * NSA_KV:3 * NSA_KV])
        vs_new = _bf16_round(new_ref[:, 3 * NSA_KV:4 * NSA_KV])
        for g in range(NSA_KV_HEADS):
            m_ref[g] = jnp.sum(qs[g] * ks_new, axis=-1, keepdims=True)
            l_ref[g] = jnp.ones((8, 1), F32)
            acc_ref[g] = jnp.broadcast_to(vs_new, (8, LANES))

    kt = [_page_t(p).astype(BF16) for p in k_pages]
    vt = [_page_t(p).astype(BF16) for p in v_pages]
    for g in range(NSA_KV_HEADS):
        q = qs[g].astype(BF16)
        live = jnp.zeros(kidx.shape, jnp.int32)
        for i in range(n_pick):
            live = jnp.where(blk == pick_ref[b, g * n_pick + i], 1, live)
        valid = jnp.broadcast_to(live > 0, (8, pg * PAGE_SIZE))
        s = jnp.concatenate([_dot(q, kt[i]) for i in range(pg)], axis=1)
        s = jnp.where(valid, s - _sample_slopes(g) * (past - kpos).astype(F32), MASKED)
        m_old = m_ref[g]
        m_new = jnp.maximum(m_old, jnp.max(s, axis=-1, keepdims=True))
        e = jnp.where(valid, jnp.exp(s - m_new), 0.0)
        alpha = jnp.exp(m_old - m_new)
        l_ref[g] = alpha * l_ref[g] + jnp.sum(e, axis=-1, keepdims=True)
        acc = alpha * acc_ref[g]
        e = e.astype(BF16)
        for i in range(pg):
            acc = acc + _dot_nt(e[:, i * PAGE_SIZE:(i + 1) * PAGE_SIZE], vt[i])
        acc_ref[g] = acc
        m_ref[g] = m_new

    @pl.when(j == pl.num_programs(1) - 1)
    def _():
        o_s = jnp.where(lane < HEAD_DIM, acc_ref[0] / l_ref[0], acc_ref[1] / l_ref[1])
        o_ref[...] = (part_ref[...] + gsel_ref[...] * o_s).astype(o_ref.dtype)


def _nsa_sample_b(cache_nsa_t, page_table, picks, qn, nkv_new, part, gsel, l, past, n_pick, pg):
    nb, n_pages = page_table.shape
    assert past % SEL_LEN == 0
    page_specs = [pl.BlockSpec((None, None, None, NSA_KV_HEADS, HEAD_DIM, PAGE_SIZE),
                               lambda b, j, pt, pk, i=i, c=c: (l, pt[b, j * pg + i], c, 0, 0, 0))
                  for c in (2, 3) for i in range(pg)]
    row = lambda w: pl.BlockSpec((None, 1, w), lambda b, j, pt, pk: (b, 0, 0))
    tile = pl.BlockSpec((None, 8, LANES), lambda b, j, pt, pk: (b, 0, 0))
    return pl.pallas_call(
        functools.partial(_nsa_sample_b_body, past=past, n_pick=n_pick, pg=pg),
        out_shape=jax.ShapeDtypeStruct((nb, 8, LANES), BF16),
        grid_spec=pltpu.PrefetchScalarGridSpec(
            num_scalar_prefetch=2,
            grid=(nb, n_pages // pg),
            in_specs=page_specs + [row(NSA_HEADS * LANES), row(4 * NSA_KV), tile, tile],
            out_specs=tile,
            scratch_shapes=[pltpu.VMEM((NSA_KV_HEADS, 8, 1), F32), pltpu.VMEM((NSA_KV_HEADS, 8, 1), F32),
                            pltpu.VMEM((NSA_KV_HEADS, 8, LANES), F32)],
        ),
        compiler_params=_cparams(2),
        name="nsa_sample_selected",
    )(page_table, picks, *([cache_nsa_t] * (2 * pg)), qn, nkv_new, part, gsel)


def _fox_sample_body(pt_ref, *refs, pg):
    pages = refs[:pg]
    lfs = refs[pg:2 * pg]
    q_ref, new_ref, sm_ref, ust_ref, aft_ref, o_ref, m_ref, l_ref, acc_ref, carry_ref = refs[2 * pg:]
    jj = pl.program_id(1)
    lane2 = lax.broadcasted_iota(jnp.int32, (8, FOX_W), 1)
    sub2 = lax.broadcasted_iota(jnp.int32, (8, FOX_W), 0)
    lane1 = lax.broadcasted_iota(jnp.int32, (8, LANES), 1)
    sub1 = lax.broadcasted_iota(jnp.int32, (8, LANES), 0)
    heads = [q_ref[:, h * LANES:(h + 1) * LANES].astype(F32) for h in range(FOX_HEADS)]
    zero = jnp.zeros((1, LANES), F32)
    even = jnp.where(lane1 < HEAD_DIM, _rows8([heads[h] if h % 2 == 0 else zero for h in range(FOX_HEADS)]), 0.0)
    odd = jnp.where(lane1 < HEAD_DIM, _rows8([heads[h] if h % 2 else zero for h in range(FOX_HEADS)]), 0.0)
    pair = even + pltpu.roll(odd, HEAD_DIM, axis=1)
    q32 = jnp.concatenate([jnp.where(sub1 // 2 == p, pair, 0.0) for p in range(FOX_HEADS // 2)], axis=1)

    def head_column(rows):
        return jnp.sum(jnp.transpose(rows), axis=-1, keepdims=True).reshape(FOX_HEADS, HEAD_DIM, 1)

    q3 = head_column(q32)

    @pl.when(jj == 0)
    def _():
        s_new = jnp.sum(q32 * new_ref[:, 0:FOX_W], axis=-1, keepdims=True)
        m_ref[...] = s_new[0:FOX_HEADS]
        l_ref[...] = jnp.ones(l_ref.shape, F32)
        v_rows = jnp.where(lane2 // HEAD_DIM == sub2, jnp.broadcast_to(new_ref[:, FOX_W:2 * FOX_W], (8, FOX_W)), 0.0)
        lane3 = lax.broadcasted_iota(jnp.int32, acc_ref.shape, 2)
        acc_ref[...] = jnp.where(lane3 == 0, head_column(v_rows), 0.0)
        carry_ref[...] = _row_to_col(sm_ref[...], SM_LOGF)

    lf = jnp.concatenate([r[...] for r in lfs], axis=0)
    ones = jnp.ones((LANES, LANES), BF16)
    within = jnp.zeros(lf.shape, F32)
    total = jnp.zeros(lf.shape, F32)
    for piece in _split3(lf):
        within = within + _dot(piece, ust_ref[...])
        total = total + _dot(piece, ones)
    after = jnp.zeros(lf.shape, F32)
    for piece in _split3(total):
        after = after + _dot(aft_ref[...], piece)
    carry = carry_ref[...]
    bias = within + after + jnp.concatenate([carry] * pg, axis=0)
    s3 = jnp.stack([jnp.sum(p[0] * q3, axis=1) for p in pages])
    s3 = s3 + bias.reshape(pg, 8, LANES)[:, 0:FOX_HEADS, :]
    m_old = m_ref[...]
    m_new = jnp.maximum(m_old, jnp.max(jnp.max(s3, axis=0), axis=-1, keepdims=True))
    e3 = jnp.exp(s3 - m_new[None])
    alpha = jnp.exp(m_old - m_new)
    l_ref[...] = alpha * l_ref[...] + jnp.sum(jnp.sum(e3, axis=0), axis=-1, keepdims=True)
    acc = alpha[:, :, None] * acc_ref[...]
    for i in range(pg):
        acc = acc + e3[i][:, None, :] * pages[i][1]
    acc_ref[...] = acc
    m_ref[...] = m_new
    carry_ref[...] = bias[0:8, 0:1] + lf[0:8, 0:1]

    @pl.when(jj == pl.num_programs(1) - 1)
    def _():
        o_ref[...] = (jnp.sum(acc_ref[...], axis=-1) / l_ref[...]).astype(o_ref.dtype)


def _fox_sample(cache_fox_t, lft, page_table, qf, fkv_new, sm, l, pg):
    nb, n_pages = page_table.shape
    steps = n_pages // pg
    ust = jnp.asarray(np.triu(np.ones((LANES, LANES), np.float32), 1).T, BF16)
    idx = np.arange(pg * 8)
    aft = ((idx[:, None] % 8 == idx[None, :] % 8) & (idx[None, :] // 8 > idx[:, None] // 8)).astype(np.float32)

    def page_of(b, j, pt, i):
        return pt[b, (steps - 1 - j) * pg + i]

    page_specs = [pl.BlockSpec((None, None, 2, FOX_HEADS, HEAD_DIM, PAGE_SIZE),
                               lambda b, j, pt, i=i: (l, page_of(b, j, pt, i), 0, 0, 0, 0)) for i in range(pg)]
    lf_specs = [pl.BlockSpec((None, None, 8, PAGE_SIZE),
                             lambda b, j, pt, i=i: (l, page_of(b, j, pt, i), 0, 0)) for i in range(pg)]
    row = lambda w: pl.BlockSpec((None, 1, w), lambda b, j, pt: (b, 0, 0))
    return pl.pallas_call(
        functools.partial(_fox_sample_body, pg=pg),
        out_shape=jax.ShapeDtypeStruct((nb, FOX_HEADS, HEAD_DIM), BF16),
        grid_spec=pltpu.PrefetchScalarGridSpec(
            num_scalar_prefetch=1,
            grid=(nb, steps),
            in_specs=page_specs + lf_specs + [
                row(FOX_HEADS * LANES), row(2 * FOX_W), row(LANES),
                _const_spec((LANES, LANES), lambda b, j, pt: (0, 0)),
                _const_spec((pg * 8, pg * 8), lambda b, j, pt: (0, 0)),
            ],
            out_specs=pl.BlockSpec((None, FOX_HEADS, HEAD_DIM), lambda b, j, pt: (b, 0, 0)),
            scratch_shapes=[pltpu.VMEM((FOX_HEADS, 1), F32), pltpu.VMEM((FOX_HEADS, 1), F32),
                            pltpu.VMEM((FOX_HEADS, HEAD_DIM, PAGE_SIZE), F32), pltpu.VMEM((8, 1), F32)],
        ),
        compiler_params=_cparams(2),
        name="fox_attention_sample",
    )(page_table, *([cache_fox_t] * pg), *([lft] * pg), qf, fkv_new, sm, ust, jnp.asarray(aft, BF16))


def _mixer_sample(x, norm_g, packed, conv_p, caches, page_table, l, tiles):
    w_proj, b_small, w_out_p, w_cmp, pe_cmp = packed
    conv_w, conv_b, ln_g, ln_b = conv_p
    cache_nsa_t, state_win, state_win_t, cache_fox_t, lft, state_conv, state_conv_t = caches
    nb = x.shape[1]
    past = page_table.shape[1] * PAGE_SIZE
    u, qn, nkv, _, _, win, _, _, qf, fkv, _, _, sm = _proj(x, norm_g, w_proj, b_small, l, nb)
    per_seq = lambda a: a.reshape(nb, 1, a.shape[-1])
    yconv = _conv_sample(state_conv_t, u[0], conv_w, conv_b, ln_g, ln_b, l)
    fs = _compress_sample(cache_nsa_t, page_table, w_cmp, pe_cmp, l, tiles['cmp_pg'])
    part, gsel, picks = _nsa_sample_a(per_seq(qn), per_seq(sm), fs, state_win_t, per_seq(win), l, past)
    n_pick = min(SEL_TOPK, -(-(past + 1) // SEL_LEN))
    picks2 = picks[:, :NSA_KV_HEADS, :n_pick].reshape(nb, NSA_KV_HEADS * n_pick)
    onsa = _nsa_sample_b(cache_nsa_t, page_table, picks2, per_seq(qn), per_seq(nkv), part, gsel, l, past, n_pick,
                         tiles['fox_pg'])
    onsa = jnp.transpose(onsa[:, :NSA_GROUP], (1, 0, 2))[None]
    ofox = _fox_sample(cache_fox_t, lft, page_table, per_seq(qf), per_seq(fkv), per_seq(sm), l, tiles['fox_pg'])
    mixed = _out_proj(x, yconv[None], onsa, ofox.reshape(1, nb, FOX_W), w_out_p, norm_g, l, nb)
    wb = state_win.shape[2]
    win_state = jnp.concatenate([state_win[l][:, 1:], win[0].reshape(nb, 1, 2, NSA_KV_HEADS, HEAD_DIM)], axis=1)
    conv_state = jnp.concatenate([state_conv[l][:, 1:], u[0][:, None, :]], axis=1)
    states = (nkv.reshape(nb, 1, 4, NSA_KV_HEADS, HEAD_DIM),
              win_state,
              fkv.reshape(nb, 1, 2, FOX_HEADS, HEAD_DIM),
              sm[0][:, None, SM_LOGF:SM_LOGF + FOX_HEADS],
              conv_state)
    return mixed, states


def _tiles(T, n_pages):
    return dict(tm=min(512, T), tc=min(512, T), nsa_tq=256, nsa_tk=min(2048, T), fox_tq=min(256, T),
                fox_tk=min(2048, T), cum_tk=min(512, T), cmp_pg=min(16, n_pages), fox_pg=min(32, n_pages))


def kernel(x_prompt, x_sample, cache_nsa_kv, state_nsa_win, cache_fox_kv, cache_fox_logf, state_conv,
           page_table, norm_g, ffn_w_gate, ffn_w_up, ffn_w_down, w_in, w_out, conv_w, conv_b,
           conv_ln_g, conv_ln_b, cmp_pe, cmp_w, nsa_gate_b, fox_forget_b):
    depth = w_in.shape[0]
    B, T, D = x_prompt.shape
    nb, ts, _ = x_sample.shape
    n_phys = cache_nsa_kv.shape[1]
    n_pages = page_table.shape[1]
    wb = state_nsa_win.shape[2]
    assert ts == 1 and D == D_MODEL and T >= WINDOW + 128 and wb == min(WINDOW, n_pages * PAGE_SIZE)
    tiles = _tiles(T, n_pages)
    assert n_pages % tiles['cmp_pg'] == 0 and n_pages % tiles['fox_pg'] == 0
    assert PAGE_SIZE % SEL_LEN == 0 and T % tiles['tm'] == 0 and T % tiles['fox_tk'] == 0 and T % tiles['nsa_tk'] == 0

    packed = _pack_params(w_in, w_out, cmp_pe, cmp_w, nsa_gate_b, fox_forget_b)
    wg, wu, wd = (w.astype(BF16) for w in (ffn_w_gate, ffn_w_up, ffn_w_down))
    ng = norm_g[:, :, None, :]
    conv_p = (conv_w, conv_b[:, None, :], conv_ln_g[:, None, :], conv_ln_b[:, None, :])
    rows_last = (0, 1, 3, 4, 5, 2)
    cache_nsa_t = jnp.transpose(cache_nsa_kv, rows_last)
    cache_fox_t = jnp.transpose(cache_fox_kv, rows_last)
    state_win_t = jnp.transpose(state_nsa_win, rows_last)
    state_conv_t = jnp.transpose(state_conv, (0, 2, 1, 3))
    lft = jnp.pad(jnp.swapaxes(cache_fox_logf, 2, 3), ((0, 0), (0, 0), (0, 8 - FOX_HEADS), (0, 0)))
    caches = (cache_nsa_t, state_nsa_win, state_win_t, cache_fox_t, lft, state_conv, state_conv_t)

    xp = x_prompt
    xs = x_sample.reshape(1, nb, D)
    p_states, s_states = [], []
    for l in range(depth):
        xp = _ffn(xp, ng, wg, wu, wd, l, 0, tiles['tm'])
        xs = _ffn(xs, ng, wg, wu, wd, l, 0, nb)
        xp, ps = _mixer_prompt(xp, ng, packed, conv_p, l, tiles)
        xs, ss = _mixer_sample(xs, ng, packed, conv_p, caches, page_table, l, tiles)
        xp = _ffn(xp, ng, wg, wu, wd, l, 1, tiles['tm'])
        xs = _ffn(xs, ng, wg, wu, wd, l, 1, nb)
        p_states.append(ps)
        s_states.append(ss)
    stack = lambda states, k: jnp.stack([s[k] for s in states])
    return (xp, xs.reshape(nb, 1, D),
            *(stack(p_states, k) for k in range(5)),
            *(stack(s_states, k) for k in range(5)))
```

```python
import functools

import jax
import jax.numpy as jnp
import numpy as np
from jax import lax
from jax.experimental import pallas as pl
from jax.experimental.pallas import tpu as pltpu

F32 = jnp.float32
BF16 = jnp.bfloat16

D_MODEL = 1024
D_FF = 2816
HEAD_DIM = 64
CONV_CH = 256
CONV_WIDTH = 31
NSA_HEADS = 8
NSA_KV_HEADS = 2
NSA_GROUP = NSA_HEADS // NSA_KV_HEADS
FOX_HEADS = 4
CMP_LEN = 32
CMP_STRIDE = 16
SEL_LEN = 64
SEL_TOPK = 16
WINDOW = 512
PAGE_SIZE = 128
RMS_EPS = 1e-6
LN_EPS = 1e-5
FORCE_SCORE = 1e9

NSA_Q = NSA_HEADS * HEAD_DIM
NSA_KV = NSA_KV_HEADS * HEAD_DIM
FOX_W = FOX_HEADS * HEAD_DIM
N_GATES = 3 * NSA_HEADS

LANES = 128
V7X_VMEM_BYTES = 64 * 1024 * 1024
VMEM_LIMIT = (V7X_VMEM_BYTES * 7) // 8

MASKED = -1e30
FF_CHUNK = 256
CONV_HALO = 32

C_GLU = 0
C_QN = C_GLU + 2 * CONV_CH
C_NKV = C_QN + NSA_Q
C_WIN = C_NKV + 4 * NSA_KV
C_QF = C_WIN + 2 * NSA_KV
C_FKV = C_QF + FOX_W
C_SM = C_FKV + 2 * FOX_W
PROJ_COLS = C_SM + LANES
SM_LOGF = N_GATES

AUX = HEAD_DIM

NT_DIMS = (((1,), (1,)), ((), ()))


def _cparams(n_grid):
    return pltpu.CompilerParams(dimension_semantics=("arbitrary",) * n_grid,
                                vmem_limit_bytes=VMEM_LIMIT)


def _const_spec(shape, index):
    return pl.BlockSpec(shape, index, pipeline_mode=pl.Buffered(1))


def _rms(x, g):
    return x * lax.rsqrt(jnp.mean(x * x, axis=-1, keepdims=True) + RMS_EPS) * g


def _split3(x):
    hi = x.astype(BF16)
    r1 = x - hi.astype(F32)
    mid = r1.astype(BF16)
    lo = (r1 - mid.astype(F32)).astype(BF16)
    return hi, mid, lo


def _dot(a, b):
    return jnp.dot(a, b, preferred_element_type=F32)


def _dot_nt(a, b):
    return lax.dot_general(a, b, NT_DIMS, preferred_element_type=F32)


def _ffn_math(x, gpre, gpost, wg_ref, wu_ref, wd_ref):
    xn = _rms(x, gpre).astype(BF16)
    acc = jnp.zeros(x.shape, F32)
    for c in range(D_FF // FF_CHUNK):
        sl = slice(c * FF_CHUNK, (c + 1) * FF_CHUNK)
        g = _dot(xn, wg_ref[:, sl])
        u = _dot(xn, wu_ref[:, sl])
        h = (g * jax.nn.sigmoid(g) * u).astype(BF16)
        acc = acc + _dot(h, wd_ref[sl, :])
    return x + 0.5 * _rms(acc, gpost)


def _ffn_body(x_ref, gpre_ref, gpost_ref, wg_ref, wu_ref, wd_ref, o_ref):
    o_ref[...] = _ffn_math(x_ref[...], gpre_ref[...], gpost_ref[...], wg_ref, wu_ref, wd_ref)


def _ffn_first(x, norm_g, wg, wu, wd, l, tm):
    B, T, D = x.shape
    return pl.pallas_call(
        _ffn_body,
        out_shape=jax.ShapeDtypeStruct(x.shape, F32),
        grid=(B, T // tm),
        in_specs=[
            pl.BlockSpec((None, tm, D), lambda b, i: (b, i, 0)),
            _const_spec((None, None, 1, D), lambda b, i: (l, 0, 0, 0)),
            _const_spec((None, None, 1, D), lambda b, i: (l, 1, 0, 0)),
            _const_spec((None, None, D, D_FF), lambda b, i: (l, 0, 0, 0)),
            _const_spec((None, None, D, D_FF), lambda b, i: (l, 0, 0, 0)),
            _const_spec((None, None, D_FF, D), lambda b, i: (l, 0, 0, 0)),
        ],
        out_specs=pl.BlockSpec((None, tm, D), lambda b, i: (b, i, 0)),
        compiler_params=_cparams(2),
        name="ffn_half_step",
    )(x, norm_g, norm_g, wg, wu, wd)


def _alibi_slope(head):
    return 2.0 ** (-8.0 * (head + 1) / NSA_HEADS)


def _store_heads(o_ref, z, fills):
    lane = lax.broadcasted_iota(jnp.int32, (z.shape[0], LANES), 1)
    for h, fill in enumerate(fills):
        blk = z[:, (h // 2) * LANES:(h // 2 + 1) * LANES]
        if h % 2:
            blk = pltpu.roll(blk, HEAD_DIM, axis=1)
        o_ref[:, h * LANES:(h + 1) * LANES] = jnp.where(lane < HEAD_DIM, blk, fill).astype(o_ref.dtype)


def _position_fill(pos, lane):
    within = (pos % SEL_LEN).astype(F32)
    block = (pos - pos % SEL_LEN).astype(F32)
    return jnp.where(lane == AUX, within, jnp.where(lane == AUX + 1, block, 0.0))


def _proj_body(x_ref, g_ref, w_ref, b_ref, u_ref, qn_ref, nkv_ref, ksel_ref, vsel_ref, win_ref, kwin_ref,
               vwin_ref, qf_ref, fkv_ref, kf_ref, vf_ref, sm_ref):
    xn = _rms(x_ref[...], g_ref[...]).astype(BF16)
    tm = xn.shape[0]
    lane = lax.broadcasted_iota(jnp.int32, (tm, LANES), 1)
    pos = pl.program_id(1) * tm + lax.broadcasted_iota(jnp.int32, (tm, LANES), 0)
    pos_fill = _position_fill(pos, lane)
    ones_fill = jnp.ones((tm, LANES), F32)
    zero_fill = jnp.zeros((tm, LANES), F32)

    def mm(lo, hi):
        return _dot(xn, w_ref[:, lo:hi])

    z = mm(C_GLU, C_QN)
    u_ref[...] = z[:, :CONV_CH] * jax.nn.sigmoid(z[:, CONV_CH:])
    slope_fills = [jnp.where((lane == AUX) | (lane == AUX + 1), _alibi_slope(h), 0.0) for h in range(NSA_HEADS)]
    _store_heads(qn_ref, mm(C_QN, C_NKV) * (HEAD_DIM ** -0.5), slope_fills)
    z = mm(C_NKV, C_WIN)
    nkv_ref[...] = z
    _store_heads(ksel_ref, z[:, 2 * NSA_KV:3 * NSA_KV], [pos_fill] * NSA_KV_HEADS)
    _store_heads(vsel_ref, z[:, 3 * NSA_KV:4 * NSA_KV], [ones_fill] * NSA_KV_HEADS)
    z = mm(C_WIN, C_QF)
    win_ref[...] = z
    _store_heads(kwin_ref, z[:, 0:NSA_KV], [pos_fill] * NSA_KV_HEADS)
    _store_heads(vwin_ref, z[:, NSA_KV:2 * NSA_KV], [ones_fill] * NSA_KV_HEADS)
    one3 = jnp.where((lane >= AUX) & (lane < AUX + 3), 1.0, 0.0)
    _store_heads(qf_ref, mm(C_QF, C_FKV) * (HEAD_DIM ** -0.5), [one3] * FOX_HEADS)
    z = mm(C_FKV, C_SM)
    fkv_ref[...] = z
    _store_heads(kf_ref, z[:, 0:FOX_W], [zero_fill] * FOX_HEADS)
    _store_heads(vf_ref, z[:, FOX_W:2 * FOX_W], [ones_fill] * FOX_HEADS)
    z = mm(C_SM, PROJ_COLS) + b_ref[...]
    log_sig = jnp.minimum(z, 0.0) - jnp.log1p(jnp.exp(-jnp.abs(z)))
    sm_ref[...] = jnp.where(lane < N_GATES, jax.nn.sigmoid(z), log_sig)


def _proj(x, norm_g, w_proj, b_small, l, tm):
    B, T, D = x.shape
    widths = [(CONV_CH, F32), (NSA_HEADS * LANES, BF16), (4 * NSA_KV, F32), (NSA_KV_HEADS * LANES, BF16),
              (NSA_KV_HEADS * LANES, BF16), (2 * NSA_KV, F32), (NSA_KV_HEADS * LANES, BF16),
              (NSA_KV_HEADS * LANES, BF16), (FOX_HEADS * LANES, BF16), (2 * FOX_W, F32),
              (FOX_HEADS * LANES, BF16), (FOX_HEADS * LANES, BF16), (LANES, F32)]
    return pl.pallas_call(
        _proj_body,
        out_shape=[jax.ShapeDtypeStruct((B, T, w), dt) for w, dt in widths],
        grid=(B, T // tm),
        in_specs=[
            pl.BlockSpec((None, tm, D), lambda b, i: (b, i, 0)),
            _const_spec((None, None, 1, D), lambda b, i: (l, 2, 0, 0)),
            _const_spec((None, D, PROJ_COLS), lambda b, i: (l, 0, 0)),
            _const_spec((None, 1, LANES), lambda b, i: (l, 0, 0)),
        ],
        out_specs=[pl.BlockSpec((None, tm, w), lambda b, i: (b, i, 0)) for w, _ in widths],
        compiler_params=_cparams(2),
        name="input_projection",
    )(x, norm_g, w_proj, b_small)


def _conv_body(prev_ref, cur_ref, w_ref, b_ref, lng_ref, lnb_ref, o_ref, hist_ref, *, tc):
    i = pl.program_id(1)
    prev = prev_ref[...]
    hist_ref[0:CONV_HALO, :] = jnp.where(i == 0, 0.0, prev)
    hist_ref[CONV_HALO:CONV_HALO + tc, :] = cur_ref[...]
    off = CONV_HALO - (CONV_WIDTH - 1)
    acc = jnp.zeros((tc, CONV_CH), F32)
    for k in range(CONV_WIDTH):
        acc = acc + hist_ref[off + k:off + k + tc, :] * w_ref[k:k + 1, :]
    o_ref[...] = _conv_post(acc, b_ref[...], lng_ref[...], lnb_ref[...]).astype(o_ref.dtype)


def _conv_post(acc, b, ln_g, ln_b):
    y = acc + b
    mu = jnp.mean(y, axis=-1, keepdims=True)
    var = jnp.mean(jnp.square(y - mu), axis=-1, keepdims=True)
    y = (y - mu) * lax.rsqrt(var + LN_EPS) * ln_g + ln_b
    return y * jax.nn.sigmoid(y)


def _conv_prompt(u, conv_w, conv_b, ln_g, ln_b, l, tc):
    B, T, C = u.shape
    halo_blocks = tc // CONV_HALO
    return pl.pallas_call(
        functools.partial(_conv_body, tc=tc),
        out_shape=jax.ShapeDtypeStruct((B, T, C), BF16),
        grid=(B, T // tc),
        in_specs=[
            pl.BlockSpec((None, CONV_HALO, C), lambda b, i: (b, jnp.maximum(i * halo_blocks - 1, 0), 0)),
            pl.BlockSpec((None, tc, C), lambda b, i: (b, i, 0)),
            _const_spec((None, CONV_WIDTH, C), lambda b, i: (l, 0, 0)),
            _const_spec((None, 1, C), lambda b, i: (l, 0, 0)),
            _const_spec((None, 1, C), lambda b, i: (l, 0, 0)),
            _const_spec((None, 1, C), lambda b, i: (l, 0, 0)),
        ],
        out_specs=pl.BlockSpec((None, tc, C), lambda b, i: (b, i, 0)),
        scratch_shapes=[pltpu.VMEM((CONV_HALO + tc, C), F32)],
        compiler_params=_cparams(2),
        name="conv_module_prompt",
    )(u, u, conv_w, conv_b, ln_g, ln_b)


def _compress_halves(load_rows, w_ref, pe_ref, n):
    first = jnp.zeros((n, 2 * NSA_KV), F32)
    second = jnp.zeros((n, 2 * NSA_KV), F32)
    for l in range(CMP_STRIDE):
        xl = load_rows(l)
        a = (xl + pe_ref[l]).astype(BF16)
        b = (xl + pe_ref[CMP_STRIDE + l]).astype(BF16)
        first = first + _dot(a, w_ref[l])
        second = second + _dot(b, w_ref[CMP_STRIDE + l])
    return first, second


def _compress_prompt_body(xk_ref, xv_ref, w_ref, pe_ref, kc_ref, vc_ref, *, n):
    def load(l):
        return jnp.concatenate([r[pl.ds(l, n, stride=CMP_STRIDE), :] for r in (xk_ref, xv_ref)], axis=1)

    first, second = _compress_halves(load, w_ref, pe_ref, n)
    out = first + pltpu.roll(second, n - 1, axis=0)
    lane = lax.broadcasted_iota(jnp.int32, (n, LANES), 1)
    start = lax.broadcasted_iota(jnp.int32, (n, LANES), 0) * CMP_STRIDE
    coarse = (start - start % 4096).astype(F32)
    fine = (start % 4096).astype(F32)
    pos_fill = jnp.where(lane == AUX, fine, jnp.where(lane == AUX + 1, coarse, 0.0))
    _store_heads(kc_ref, out[:, 0:NSA_KV], [pos_fill] * NSA_KV_HEADS)
    _store_heads(vc_ref, out[:, NSA_KV:2 * NSA_KV], [jnp.ones((n, LANES), F32)] * NSA_KV_HEADS)


def _compress_prompt(nkv, w_cmp, pe_cmp, l):
    B, T, _ = nkv.shape
    n = T // CMP_STRIDE
    aug = jax.ShapeDtypeStruct((B, n, NSA_KV_HEADS * LANES), BF16)
    return pl.pallas_call(
        functools.partial(_compress_prompt_body, n=n),
        out_shape=[aug, aug],
        grid=(B,),
        in_specs=[
            pl.BlockSpec((None, T, NSA_KV), lambda b: (b, 0, 0)),
            pl.BlockSpec((None, T, NSA_KV), lambda b: (b, 0, 1)),
            _const_spec((None, CMP_LEN, 2 * NSA_KV, 2 * NSA_KV), lambda b: (l, 0, 0, 0)),
            _const_spec((None, CMP_LEN, 1, 2 * NSA_KV), lambda b: (l, 0, 0, 0)),
        ],
        out_specs=[pl.BlockSpec((None, n, NSA_KV_HEADS * LANES), lambda b: (b, 0, 0))] * 2,
        compiler_params=_cparams(1),
        name="nsa_compress_prompt",
    )(nkv, nkv, w_cmp, pe_cmp)


def _cumsum_body(sm_ref, kf_ref, tri_ref, o_ref, carry_ref):
    @pl.when(pl.program_id(1) == 0)
    def _():
        carry_ref[...] = jnp.zeros_like(carry_ref)

    tri = tri_ref[...]
    c = carry_ref[...]
    for part in _split3(sm_ref[...]):
        c = c + _dot(tri, part)
    tk = c.shape[0]
    carry_ref[...] = c[tk - 1:, :]
    lane = lax.broadcasted_iota(jnp.int32, (tk, LANES), 1)
    for h in range(FOX_HEADS):
        neg = jnp.broadcast_to(-c[:, SM_LOGF + h:SM_LOGF + h + 1], (tk, LANES))
        hi, mid, lo = (part.astype(F32) for part in _split3(neg))
        aux = jnp.where(lane == AUX, hi, jnp.where(lane == AUX + 1, mid, jnp.where(lane == AUX + 2, lo, 0.0)))
        blk = slice(h * LANES, (h + 1) * LANES)
        o_ref[:, blk] = (kf_ref[:, blk].astype(F32) + aux).astype(o_ref.dtype)


def _cumsum(sm, kf, tk):
    B, T, _ = sm.shape
    tri = jnp.asarray(np.tril(np.ones((tk, tk), np.float32)), BF16)
    return pl.pallas_call(
        _cumsum_body,
        out_shape=jax.ShapeDtypeStruct(kf.shape, BF16),
        grid=(B, T // tk),
        in_specs=[
            pl.BlockSpec((None, tk, LANES), lambda b, i: (b, i, 0)),
            pl.BlockSpec((None, tk, FOX_HEADS * LANES), lambda b, i: (b, i, 0)),
            _const_spec((tk, tk), lambda b, i: (0, 0)),
        ],
        out_specs=pl.BlockSpec((None, tk, FOX_HEADS * LANES), lambda b, i: (b, i, 0)),
        scratch_shapes=[pltpu.VMEM((1, LANES), F32)],
        compiler_params=_cparams(2),
        name="fox_cumsum",
    )(sm, kf, tri)


def _online_update(s, v, m_ref, acc_ref, idx):
    m_old = m_ref[idx]
    m_new = jnp.maximum(m_old, jnp.max(s, axis=-1, keepdims=True))
    e = jnp.exp(s - m_new)
    acc_ref[idx] = jnp.exp(m_old - m_new) * acc_ref[idx] + _dot(e.astype(BF16), v)
    m_ref[idx] = m_new


def _normalized(acc):
    return acc / acc[:, AUX:AUX + 1]


def _pair_lanes(even, odd):
    lane = lax.broadcasted_iota(jnp.int32, even.shape, 1)
    return jnp.where(lane < HEAD_DIM, even, pltpu.roll(odd, HEAD_DIM, axis=1))


def _fox_body(q_ref, k_ref, v_ref, o_ref, m_ref, acc_ref, *, tq, tk):
    t0 = pl.program_id(1) * tq
    last = (t0 + tq - 1) // tk
    qpos = t0 + lax.broadcasted_iota(jnp.int32, (tq, 1), 0)
    qs = [q_ref[:, h * LANES:(h + 1) * LANES] for h in range(FOX_HEADS)]
    m_ref[...] = jnp.full(m_ref.shape, MASKED, F32)
    acc_ref[...] = jnp.zeros(acc_ref.shape, F32)

    def tile(j, causal):
        ks = pl.multiple_of(j * tk, tk)
        for h in range(FOX_HEADS):
            blk = slice(h * LANES, (h + 1) * LANES)
            s = _dot_nt(qs[h], k_ref[pl.ds(ks, tk), blk])
            if causal:
                kpos = ks + lax.broadcasted_iota(jnp.int32, (1, tk), 1)
                s = jnp.where(kpos <= qpos, s, MASKED)
            _online_update(s, v_ref[pl.ds(ks, tk), blk], m_ref, acc_ref, h)

    def full_tile(j, carry):
        tile(j, False)
        return carry

    lax.fori_loop(0, last, full_tile, 0)
    tile(last, True)
    for p in range(FOX_HEADS // 2):
        o_ref[:, p * LANES:(p + 1) * LANES] = _pair_lanes(
            _normalized(acc_ref[2 * p]), _normalized(acc_ref[2 * p + 1])).astype(o_ref.dtype)


def _batch_spec(shape):
    return pl.BlockSpec((None,) + shape, lambda b, i: (b,) + (0,) * len(shape), pipeline_mode=pl.Buffered(1))


def _fox_prompt(qf, kf, vf, tq, tk):
    B, T, _ = qf.shape
    assert tk % tq == 0
    return pl.pallas_call(
        functools.partial(_fox_body, tq=tq, tk=tk),
        out_shape=jax.ShapeDtypeStruct((B, T, FOX_W), BF16),
        grid=(B, T // tq),
        in_specs=[
            pl.BlockSpec((None, tq, FOX_HEADS * LANES), lambda b, i: (b, i, 0)),
            _batch_spec((T, FOX_HEADS * LANES)),
            _batch_spec((T, FOX_HEADS * LANES)),
        ],
        out_specs=pl.BlockSpec((None, tq, FOX_W), lambda b, i: (b, i, 0)),
        scratch_shapes=[pltpu.VMEM((FOX_HEADS, tq, 1), F32), pltpu.VMEM((FOX_HEADS, tq, LANES), F32)],
        compiler_params=_cparams(2),
        name="fox_attention_prompt",
    )(qf, kf, vf)


def _masked_softmax_rows(s, valid):
    s = jnp.where(valid, s, MASKED)
    m = jnp.max(s, axis=-1, keepdims=True)
    e = jnp.where(valid, jnp.exp(s - m), 0.0)
    d = jnp.sum(e, axis=-1, keepdims=True)
    return e / jnp.where(d > 0, d, 1.0)


def _importance_matrix(nc, ns):
    ratio = SEL_LEN // CMP_STRIDE
    m = np.zeros((nc, ns), np.float32)
    for j in range(ns):
        for i in range(ratio * j, ratio * j + ratio):
            for src in (i, i - 1):
                if 0 <= src < nc:
                    m[src, j] += 1.0
    return m


def _topk_columns(score, n_pick):
    n_blk = score.shape[0]
    blk = lax.broadcasted_iota(jnp.int32, score.shape, 0).astype(F32)
    picked = jnp.zeros(score.shape, jnp.bool_)
    for _ in range(n_pick):
        m = jnp.max(score, axis=0, keepdims=True)
        first = jnp.min(jnp.where(score == m, blk, float(n_blk)), axis=0, keepdims=True)
        hit = blk == first
        picked = picked | hit
        score = jnp.where(hit, -jnp.inf, score)
    return picked


def _nsa_body(q_ref, sm_ref, kc_ref, vc_ref, ksel_ref, vsel_ref, kwin_ref, vwin_ref, ind_ref, impt_ref,
              o_ref, m_ref, acc_ref, *, tq, tk, wk):
    t0 = pl.program_id(1) * tq
    T = ksel_ref.shape[0]
    nc = kc_ref.shape[0]
    ns = impt_ref.shape[0]
    R = NSA_GROUP
    G = NSA_KV_HEADS
    last = (t0 + tq - 1) // tk
    qpos = t0 + lax.broadcasted_iota(jnp.int32, (tq, 1), 0)
    qpos_r = jnp.concatenate([qpos] * R, axis=0)
    gates = sm_ref[...]
    qa = [jnp.concatenate([q_ref[:, (g * R + r) * LANES:(g * R + r + 1) * LANES] for r in range(R)], axis=0)
          for g in range(G)]

    c_end = lax.broadcasted_iota(jnp.int32, (1, nc), 1) * CMP_STRIDE + (CMP_LEN - 1)
    valid_c = c_end <= qpos_r
    o_c, imp_t = [], []
    for g in range(G):
        blk_g = slice(g * LANES, (g + 1) * LANES)
        p_c = _masked_softmax_rows(_dot_nt(qa[g], kc_ref[:, blk_g]), valid_c)
        o_c.append(_dot(p_c.astype(BF16), vc_ref[:, blk_g]))
        imp = p_c[0:tq]
        for r in range(1, R):
            imp = imp + p_c[r * tq:(r + 1) * tq]
        acc = jnp.zeros((ns, tq), F32)
        for part in _split3(imp):
            acc = acc + _dot_nt(impt_ref[...], part)
        imp_t.append(acc)
    imp_t = jnp.concatenate(imp_t, axis=1)
    blk = lax.broadcasted_iota(jnp.int32, imp_t.shape, 0)
    cur = (t0 + lax.broadcasted_iota(jnp.int32, imp_t.shape, 1) % tq) // SEL_LEN
    forced = (blk == 0) | (blk == cur) | (blk == cur - 1)
    score = jnp.where(forced, FORCE_SCORE, imp_t)
    score = jnp.where(blk > cur, -3.0e38, score)
    picked = _topk_columns(score, min(SEL_TOPK, ns)) & (blk <= cur)
    veto = jnp.where(picked, 0.0, MASKED)

    outs = []
    for g in range(G):
        blk_g = slice(g * LANES, (g + 1) * LANES)
        veto_g = jnp.transpose(veto[:, g * tq:(g + 1) * tq]).astype(BF16)
        q_aug = jnp.concatenate([jnp.concatenate([veto_g] * R, axis=0), qa[g]], axis=1)
        m_ref[...] = jnp.full(m_ref.shape, MASKED, F32)
        acc_ref[...] = jnp.zeros(acc_ref.shape, F32)

        def tile(j, causal, blk_g=blk_g, q_aug=q_aug):
            ks = pl.multiple_of(j * tk, tk)
            k_aug = jnp.concatenate([ind_ref[pl.ds(ks, tk), :], ksel_ref[pl.ds(ks, tk), blk_g]], axis=1)
            s = _dot_nt(q_aug, k_aug)
            if causal:
                kpos = ks + lax.broadcasted_iota(jnp.int32, (1, tk), 1)
                s = jnp.where(kpos <= qpos_r, s, MASKED)
            _online_update(s, vsel_ref[pl.ds(ks, tk), blk_g], m_ref, acc_ref, 0)

        def full_tile(j, carry, tile=tile):
            tile(j, False)
            return carry

        lax.fori_loop(0, last, full_tile, 0)
        tile(last, True)
        o_s = _normalized(acc_ref[0])

        ws = pl.multiple_of(jnp.clip(t0 - WINDOW, 0, T - wk), LANES)
        dist_w = qpos_r - (ws + lax.broadcasted_iota(jnp.int32, (1, wk), 1))
        s_w = _dot_nt(qa[g], kwin_ref[pl.ds(ws, wk), blk_g])
        s_w = jnp.where((dist_w >= 0) & (dist_w <= WINDOW), s_w, MASKED)
        e_w = jnp.exp(s_w - jnp.max(s_w, axis=-1, keepdims=True))
        o_w = _normalized(_dot(e_w.astype(BF16), vwin_ref[pl.ds(ws, wk), blk_g]))

        def gate(j, g=g):
            cols = [jnp.broadcast_to(gates[:, (g * R + r) * 3 + j:(g * R + r) * 3 + j + 1], (tq, LANES))
                    for r in range(R)]
            return jnp.concatenate(cols, axis=0)

        outs.append(gate(0) * o_c[g] + gate(1) * o_s + gate(2) * o_w)
    out = _pair_lanes(outs[0], outs[1])
    for r in range(R):
        o_ref[r] = out[r * tq:(r + 1) * tq].astype(o_ref.dtype)


def _nsa_prompt(qn, sm, kc, vc, ksel, vsel, kwin, vwin, tq, tk):
    B, T, _ = qn.shape
    nc = kc.shape[1]
    ns = T // SEL_LEN
    wk = WINDOW + tq
    assert tk % tq == 0 and T >= wk
    impt = jnp.asarray(_importance_matrix(nc, ns).T, BF16)
    ind = jnp.asarray(np.arange(T)[:, None] // SEL_LEN == np.arange(ns)[None, :], BF16)
    width = NSA_KV_HEADS * LANES
    return pl.pallas_call(
        functools.partial(_nsa_body, tq=tq, tk=tk, wk=wk),
        out_shape=jax.ShapeDtypeStruct((B, NSA_GROUP, T, LANES), BF16),
        grid=(B, T // tq),
        in_specs=[
            pl.BlockSpec((None, tq, NSA_HEADS * LANES), lambda b, i: (b, i, 0)),
            pl.BlockSpec((None, tq, LANES), lambda b, i: (b, i, 0)),
            _batch_spec((nc, width)), _batch_spec((nc, width)),
            _batch_spec((T, width)), _batch_spec((T, width)),
            _batch_spec((T, width)), _batch_spec((T, width)),
            _const_spec((T, ns), lambda b, i: (0, 0)),
            _const_spec((ns, nc), lambda b, i: (0, 0)),
        ],
        out_specs=pl.BlockSpec((None, NSA_GROUP, tq, LANES), lambda b, i: (b, 0, i, 0)),
        scratch_shapes=[pltpu.VMEM((1, NSA_GROUP * tq, 1), F32), pltpu.VMEM((1, NSA_GROUP * tq, LANES), F32)],
        compiler_params=_cparams(2),
        name="nsa_attention_prompt",
    )(qn, sm, kc, vc, ksel, vsel, kwin, vwin, ind, impt)


def _out_body(x_ref, yc_ref, on_ref, of_ref, w_ref, g_ref, gpre_ref, gpost_ref, wg_ref, wu_ref, wd_ref, o_ref):
    cat = jnp.concatenate([yc_ref[...]] + [on_ref[r] for r in range(NSA_GROUP)] + [of_ref[...]], axis=1)
    y = _dot(cat, w_ref[...])
    mixed = x_ref[...] + _rms(y, g_ref[...])
    o_ref[...] = _ffn_math(mixed, gpre_ref[...], gpost_ref[...], wg_ref, wu_ref, wd_ref)


def _out_proj_ffn(x, yconv, onsa, ofox, w_out_p, norm_g, wg, wu, wd, l, tm):
    B, T, D = x.shape
    return pl.pallas_call(
        _out_body,
        out_shape=jax.ShapeDtypeStruct(x.shape, F32),
        grid=(B, T // tm),
        in_specs=[
            pl.BlockSpec((None, tm, D), lambda b, i: (b, i, 0)),
            pl.BlockSpec((None, tm, CONV_CH), lambda b, i: (b, i, 0)),
            pl.BlockSpec((None, NSA_GROUP, tm, LANES), lambda b, i: (b, 0, i, 0)),
            pl.BlockSpec((None, tm, FOX_W), lambda b, i: (b, i, 0)),
            _const_spec((None, D, D), lambda b, i: (l, 0, 0)),
            _const_spec((None, None, 1, D), lambda b, i: (l, 3, 0, 0)),
            _const_spec((None, None, 1, D), lambda b, i: (l, 4, 0, 0)),
            _const_spec((None, None, 1, D), lambda b, i: (l, 5, 0, 0)),
            _const_spec((None, None, D, D_FF), lambda b, i: (l, 1, 0, 0)),
            _const_spec((None, None, D, D_FF), lambda b, i: (l, 1, 0, 0)),
            _const_spec((None, None, D_FF, D), lambda b, i: (l, 1, 0, 0)),
        ],
        out_specs=pl.BlockSpec((None, tm, D), lambda b, i: (b, i, 0)),
        compiler_params=_cparams(2),
        name="output_projection_ffn",
    )(x, yconv, onsa, ofox, w_out_p, norm_g, norm_g, norm_g, wg, wu, wd)


def _proj_column_index():
    in_cols = 2 * CONV_CH + NSA_Q + 6 * NSA_KV + N_GATES + 3 * FOX_W + FOX_HEADS
    src_qn = 2 * CONV_CH
    src_kv = src_qn + NSA_Q
    src_gate = src_kv + 6 * NSA_KV
    src_fox = src_gate + N_GATES
    src_ff = src_fox + 3 * FOX_W
    idx = np.full((PROJ_COLS,), in_cols, np.int32)
    idx[C_GLU:C_QN] = np.arange(2 * CONV_CH)
    idx[C_QN:C_NKV] = src_qn + np.arange(NSA_Q)
    idx[C_NKV:C_QF] = src_kv + np.arange(6 * NSA_KV)
    idx[C_QF:C_FKV] = src_fox + np.arange(FOX_W)
    idx[C_FKV:C_SM] = src_fox + FOX_W + np.arange(2 * FOX_W)
    idx[C_SM:C_SM + N_GATES] = src_gate + np.arange(N_GATES)
    idx[C_SM + N_GATES:C_SM + N_GATES + FOX_HEADS] = src_ff + np.arange(FOX_HEADS)
    return idx


def _out_row_index():
    idx = np.arange(D_MODEL, dtype=np.int32)
    for r in range(NSA_GROUP):
        for g in range(NSA_KV_HEADS):
            dst = CONV_CH + r * LANES + g * HEAD_DIM
            src = CONV_CH + (g * NSA_GROUP + r) * HEAD_DIM
            idx[dst:dst + HEAD_DIM] = src + np.arange(HEAD_DIM)
    return idx


def _pack_params(w_in, w_out, cmp_pe, cmp_w, nsa_gate_b, fox_forget_b):
    depth = w_in.shape[0]
    w_ext = jnp.concatenate([w_in, jnp.zeros((depth, D_MODEL, 1), w_in.dtype)], axis=-1)
    w_proj = jnp.take(w_ext, jnp.asarray(_proj_column_index()), axis=-1).astype(BF16)
    b_small = jnp.concatenate([nsa_gate_b, fox_forget_b,
                               jnp.zeros((depth, LANES - N_GATES - FOX_HEADS), F32)], axis=-1)[:, None, :]
    w_out_p = jnp.take(w_out, jnp.asarray(_out_row_index()), axis=1).astype(BF16)
    w3 = cmp_w.reshape(depth, 2, CMP_LEN, HEAD_DIM, HEAD_DIM)
    eye_g = jnp.eye(NSA_KV_HEADS, dtype=F32)
    blocks = [jnp.einsum('ab,xlde->xladbe', eye_g, w3[:, kv]).reshape(depth, CMP_LEN, NSA_KV, NSA_KV)
              for kv in range(2)]
    zero = jnp.zeros_like(blocks[0])
    w_cmp = jnp.concatenate([jnp.concatenate([blocks[0], zero], axis=-1),
                             jnp.concatenate([zero, blocks[1]], axis=-1)], axis=-2).astype(BF16)
    pe = jnp.concatenate([cmp_pe[:, 0], cmp_pe[:, 0], cmp_pe[:, 1], cmp_pe[:, 1]], axis=-1)
    return w_proj, b_small, w_out_p, w_cmp, pe[:, :, None, :]


def _mixer_prompt(x, norm_g, packed, conv_p, ffn_w, l, tiles):
    w_proj, b_small, w_out_p, w_cmp, pe_cmp = packed
    conv_w, conv_b, ln_g, ln_b = conv_p
    B, T, _ = x.shape
    u, qn, nkv, ksel, vsel, win, kwin, vwin, qf, fkv, kf, vf, sm = _proj(x, norm_g, w_proj, b_small, l, tiles['tm'])
    yconv = _conv_prompt(u, conv_w, conv_b, ln_g, ln_b, l, tiles['tc'])
    kc, vc = _compress_prompt(nkv, w_cmp, pe_cmp, l)
    onsa = _nsa_prompt(qn, sm, kc, vc, ksel, vsel, kwin, vwin, tiles['nsa_tq'], tiles['nsa_tk'])
    kf = _cumsum(sm, kf, tiles['cum_tk'])
    ofox = _fox_prompt(qf, kf, vf, tiles['fox_tq'], tiles['fox_tk'])
    mixed = _out_proj_ffn(x, yconv, onsa, ofox, w_out_p, norm_g, *ffn_w, l, tiles['tm'])
    states = (nkv.reshape(B, T, 4, NSA_KV_HEADS, HEAD_DIM),
              win[:, T - WINDOW:].reshape(B, WINDOW, 2, NSA_KV_HEADS, HEAD_DIM),
              fkv.reshape(B, T, 2, FOX_HEADS, HEAD_DIM),
              sm[:, :, SM_LOGF:SM_LOGF + FOX_HEADS],
              u[:, T - (CONV_WIDTH - 1):])
    return mixed, states


def _row_to_col(row, first_lane, stride=1):
    lane = lax.broadcasted_iota(jnp.int32, (8, LANES), 1)
    sub = lax.broadcasted_iota(jnp.int32, (8, LANES), 0)
    picked = jnp.where(lane == first_lane + stride * sub, jnp.broadcast_to(row, (8, LANES)), 0.0)
    return jnp.sum(picked, axis=-1, keepdims=True)


def _rows8(blocks):
    n = blocks[0].shape[1]
    sub = lax.broadcasted_iota(jnp.int32, (8, n), 0)
    out = jnp.zeros((8, n), F32)
    for r, blk in enumerate(blocks):
        out = jnp.where(sub == r, jnp.broadcast_to(blk, (8, n)), out)
    return out


def _conv_sample_body(st_ref, u_ref, w_ref, b_ref, lng_ref, lnb_ref, o_ref):
    acc = u_ref[...] * w_ref[CONV_WIDTH - 1:CONV_WIDTH, :]
    for k in range(CONV_WIDTH - 1):
        acc = acc + st_ref[k] * w_ref[k:k + 1, :]
    o_ref[...] = _conv_post(acc, b_ref[...], lng_ref[...], lnb_ref[...]).astype(o_ref.dtype)


def _conv_sample(state_conv_t, u, conv_w, conv_b, ln_g, ln_b, l):
    nb, C = u.shape
    hist = CONV_WIDTH - 1
    return pl.pallas_call(
        _conv_sample_body,
        out_shape=jax.ShapeDtypeStruct((nb, C), BF16),
        grid=(1,),
        in_specs=[
            pl.BlockSpec((None, hist, nb, C), lambda i: (l, 0, 0, 0)),
            pl.BlockSpec((nb, C), lambda i: (0, 0)),
            pl.BlockSpec((None, CONV_WIDTH, C), lambda i: (l, 0, 0)),
            pl.BlockSpec((None, 1, C), lambda i: (l, 0, 0)),
            pl.BlockSpec((None, 1, C), lambda i: (l, 0, 0)),
            pl.BlockSpec((None, 1, C), lambda i: (l, 0, 0)),
        ],
        out_specs=pl.BlockSpec((nb, C), lambda i: (0, 0)),
        compiler_params=_cparams(1),
        name="conv_module_sample",
    )(state_conv_t, u, conv_w, conv_b, ln_g, ln_b)


def _page_t(ref):
    v = ref[...]
    return v.reshape(v.shape[0] * v.shape[1], v.shape[2])


def _compress_sample_body(pt_ref, *refs, pg):
    pages = refs[:pg]
    w_ref, pe_ref, o_ref, xk_ref, xv_ref = refs[pg:]
    rows = PAGE_SIZE // CMP_STRIDE
    n = pg * rows
    for i in range(pg):
        xk_ref[i * PAGE_SIZE:(i + 1) * PAGE_SIZE, :] = jnp.transpose(_page_t(pages[i].at[0]))
        xv_ref[i * PAGE_SIZE:(i + 1) * PAGE_SIZE, :] = jnp.transpose(_page_t(pages[i].at[1]))

    def load(l):
        return jnp.concatenate([r[pl.ds(l, n, stride=CMP_STRIDE), :] for r in (xk_ref, xv_ref)], axis=1)

    first, second = _compress_halves(load, w_ref, pe_ref, n)
    o_ref[:, 0:2 * NSA_KV] = first
    o_ref[:, 2 * NSA_KV:4 * NSA_KV] = second


def _compress_sample(cache_nsa_t, page_table, w_cmp, pe_cmp, l, pg):
    nb, n_pages = page_table.shape
    rows = PAGE_SIZE // CMP_STRIDE
    page_specs = [
        pl.BlockSpec((None, None, 2, NSA_KV_HEADS, HEAD_DIM, PAGE_SIZE),
                     lambda b, j, pt, i=i: (l, pt[b, j * pg + i], 0, 0, 0, 0))
        for i in range(pg)]
    return pl.pallas_call(
        functools.partial(_compress_sample_body, pg=pg),
        out_shape=jax.ShapeDtypeStruct((nb, n_pages * rows, 4 * NSA_KV), F32),
        grid_spec=pltpu.PrefetchScalarGridSpec(
            num_scalar_prefetch=1,
            grid=(nb, n_pages // pg),
            in_specs=page_specs + [
                _const_spec((None, CMP_LEN, 2 * NSA_KV, 2 * NSA_KV), lambda b, j, pt: (l, 0, 0, 0)),
                _const_spec((None, CMP_LEN, 1, 2 * NSA_KV), lambda b, j, pt: (l, 0, 0, 0)),
            ],
            out_specs=pl.BlockSpec((None, pg * rows, 4 * NSA_KV), lambda b, j, pt: (b, j, 0)),
            scratch_shapes=[pltpu.VMEM((pg * PAGE_SIZE, NSA_KV), F32), pltpu.VMEM((pg * PAGE_SIZE, NSA_KV), F32)],
        ),
        compiler_params=_cparams(2),
        name="nsa_compress_sample",
    )(page_table, *([cache_nsa_t] * pg), w_cmp, pe_cmp)


def _sample_q(q_ref, g):
    rows = [q_ref[:, (g * NSA_GROUP + r) * LANES:(g * NSA_GROUP + r + 1) * LANES].astype(F32)
            for r in range(NSA_GROUP)]
    tile = _rows8(rows)
    lane = lax.broadcasted_iota(jnp.int32, tile.shape, 1)
    tile = jnp.where(lane < HEAD_DIM, tile, 0.0)
    return pltpu.roll(tile, HEAD_DIM, axis=1) if g else tile


def _bf16_round(x):
    return x.astype(BF16).astype(F32)


def _sample_slopes(g):
    sub = lax.broadcasted_iota(jnp.int32, (8, 1), 0)
    slope = jnp.zeros((8, 1), F32)
    for r in range(NSA_GROUP):
        slope = jnp.where(sub == r, 2.0 ** (-8.0 * (g * NSA_GROUP + r + 1) / NSA_HEADS), slope)
    return slope


def _softmax_with_new(s_past, valid, s_new):
    s_past = jnp.where(valid, s_past, MASKED)
    m = jnp.maximum(jnp.max(s_past, axis=-1, keepdims=True), s_new)
    e = jnp.where(valid, jnp.exp(s_past - m), 0.0)
    e_new = jnp.exp(s_new - m)
    d = jnp.sum(e, axis=-1, keepdims=True) + e_new
    return e / d, e_new / d


def _nsa_sample_a_body(q_ref, sm_ref, fs_ref, kw_ref, vw_ref, winn_ref, impm_ref, part_ref, gsel_ref, pick_ref,
                       *, past, ns):
    nc = fs_ref.shape[0]
    wb = kw_ref.shape[2]
    nsp = impm_ref.shape[1]
    lane = lax.broadcasted_iota(jnp.int32, (8, LANES), 1)
    sub = lax.broadcasted_iota(jnp.int32, (8, LANES), 0)
    kcvc = (fs_ref[:, 0:2 * NSA_KV] + pltpu.roll(fs_ref[:, 2 * NSA_KV:4 * NSA_KV], nc - 1, axis=0)).astype(BF16)
    kw_t = _page_t(kw_ref).astype(BF16)
    vw_t = _page_t(vw_ref).astype(BF16)
    kw_new = _bf16_round(winn_ref[:, 0:NSA_KV])
    vw_new = _bf16_round(winn_ref[:, NSA_KV:2 * NSA_KV])
    sm_row = sm_ref[...]
    part = None
    gsel = None
    picks = jnp.zeros((8, LANES), jnp.int32)
    for g in range(NSA_KV_HEADS):
        q32 = _sample_q(q_ref, g)
        q = q32.astype(BF16)
        slope = _sample_slopes(g)
        c_end = lax.broadcasted_iota(jnp.int32, (1, nc), 1) * CMP_STRIDE + (CMP_LEN - 1)
        dist_c = past - c_end
        s_c = _dot_nt(q, kcvc[:, 0:LANES]) - slope * dist_c.astype(F32)
        p_c = _masked_softmax_rows(s_c, jnp.broadcast_to(dist_c >= 0, s_c.shape))
        o_c = _dot(p_c.astype(BF16), kcvc[:, LANES:2 * LANES])
        sub_c = lax.broadcasted_iota(jnp.int32, p_c.shape, 0)
        imp = jnp.sum(jnp.where(sub_c < NSA_GROUP, p_c, 0.0), axis=0, keepdims=True)
        imp = jnp.broadcast_to(imp, p_c.shape)
        imp_sel = jnp.zeros((8, nsp), F32)
        for piece in _split3(imp):
            imp_sel = imp_sel + _dot(piece, impm_ref[...])
        blk = lax.broadcasted_iota(jnp.int32, (8, nsp), 1)
        cur = past // SEL_LEN
        forced = (blk == 0) | (blk == cur) | (blk == cur - 1)
        score = jnp.where(forced, FORCE_SCORE, imp_sel)
        score = jnp.where((blk > cur) | (blk >= ns), -3.0e38, score)
        blk_f = blk.astype(F32)
        for it in range(min(SEL_TOPK, ns)):
            m = jnp.max(score, axis=-1, keepdims=True)
            first = jnp.min(jnp.where(score == m, blk_f, float(nsp)), axis=-1, keepdims=True)
            score = jnp.where(blk_f == first, -jnp.inf, score)
            picks = jnp.where((lane == it) & (sub == g), first.astype(jnp.int32), picks)
        dist_w = wb - lax.broadcasted_iota(jnp.int32, (1, wb), 1)
        s_w = _dot(q, kw_t) - slope * dist_w.astype(F32)
        s_new = jnp.sum(q32 * kw_new, axis=-1, keepdims=True)
        p_w, p_new = _softmax_with_new(s_w, jnp.broadcast_to(dist_w <= WINDOW, s_w.shape), s_new)
        o_w = _dot_nt(p_w.astype(BF16), vw_t) + _bf16_round(p_new) * vw_new
        g_c = _row_to_col(sm_row, g * NSA_GROUP * 3 + 0, 3)
        g_s = _row_to_col(sm_row, g * NSA_GROUP * 3 + 1, 3)
        g_w = _row_to_col(sm_row, g * NSA_GROUP * 3 + 2, 3)
        pg_ = g_c * o_c + g_w * o_w
        gs_ = jnp.broadcast_to(g_s, (8, LANES))
        part = pg_ if part is None else jnp.where(lane < g * HEAD_DIM, part, pg_)
        gsel = gs_ if gsel is None else jnp.where(lane < g * HEAD_DIM, gsel, gs_)
    part_ref[...] = part
    gsel_ref[...] = gsel
    pick_ref[...] = picks


def _nsa_sample_a(qn, sm, fs, state_win_t, win_new, l, past):
    nb = qn.shape[0]
    nc = fs.shape[1]
    wb = state_win_t.shape[5]
    win_spec = lambda kv: pl.BlockSpec((None, None, None, NSA_KV_HEADS, HEAD_DIM, wb),
                                       lambda b: (l, b, kv, 0, 0, 0))
    ns = -(-(past + 1) // SEL_LEN)
    nsp = -(-ns // LANES) * LANES
    impm = jnp.asarray(_importance_matrix(nc, nsp), BF16)
    row = lambda w: pl.BlockSpec((None, 1, w), lambda b: (b, 0, 0))
    tile = pl.BlockSpec((None, 8, LANES), lambda b: (b, 0, 0))
    return pl.pallas_call(
        functools.partial(_nsa_sample_a_body, past=past, ns=ns),
        out_shape=[jax.ShapeDtypeStruct((nb, 8, LANES), F32), jax.ShapeDtypeStruct((nb, 8, LANES), F32),
                   jax.ShapeDtypeStruct((nb, 8, LANES), jnp.int32)],
        grid=(nb,),
        in_specs=[
            row(NSA_HEADS * LANES), row(LANES),
            pl.BlockSpec((None, nc, 4 * NSA_KV), lambda b: (b, 0, 0)),
            win_spec(0), win_spec(1),
            row(2 * NSA_KV),
            _const_spec((nc, nsp), lambda b: (0, 0)),
        ],
        out_specs=[tile, tile, tile],
        compiler_params=_cparams(1),
        name="nsa_sample_compressed_window",
    )(qn, sm, fs, state_win_t, state_win_t, win_new, impm)


def _nsa_sample_b_body(pt_ref, pick_ref, *refs, past, n_pick, pg):
    pages = refs[:pg]
    q_ref, new_ref, part_ref, gsel_ref, o_ref, m_ref, l_ref, acc_ref = refs[pg:]
    b = pl.program_id(0)
    j = pl.program_id(1)
    lane = lax.broadcasted_iota(jnp.int32, (8, LANES), 1)
    kidx = lax.broadcasted_iota(jnp.int32, (1, pg * PAGE_SIZE), 1)
    kpos = j * (pg * PAGE_SIZE) + kidx
    blk = kpos // SEL_LEN
    qs = [_sample_q(q_ref, g) for g in range(NSA_KV_HEADS)]

    @pl.when(j == 0)
    def _():
        ks_new = _bf16_round(new_ref[:, 2 * NSA_KV:3 * NSA_KV])
        vs_new = _bf16_round(new_ref[:, 3 * NSA_KV:4 * NSA_KV])
        for g in range(NSA_KV_HEADS):
            m_ref[g] = jnp.sum(qs[g] * ks_new, axis=-1, keepdims=True)
            l_ref[g] = jnp.ones((8, 1), F32)
            acc_ref[g] = jnp.broadcast_to(vs_new, (8, LANES))

    kt = [_page_t(p.at[0]).astype(BF16) for p in pages]
    vt = [_page_t(p.at[1]).astype(BF16) for p in pages]
    for g in range(NSA_KV_HEADS):
        q = qs[g].astype(BF16)
        live = jnp.zeros(kidx.shape, jnp.int32)
        for i in range(n_pick):
            live = jnp.where(blk == pick_ref[b, g * n_pick + i], 1, live)
        valid = jnp.broadcast_to(live > 0, (8, pg * PAGE_SIZE))
        s = jnp.concatenate([_dot(q, kt[i]) for i in range(pg)], axis=1)
        s = jnp.where(valid, s - _sample_slopes(g) * (past - kpos).astype(F32), MASKED)
        m_old = m_ref[g]
        m_new = jnp.maximum(m_old, jnp.max(s, axis=-1, keepdims=True))
        e = jnp.where(valid, jnp.exp(s - m_new), 0.0)
        alpha = jnp.exp(m_old - m_new)
        l_ref[g] = alpha * l_ref[g] + jnp.sum(e, axis=-1, keepdims=True)
        acc = alpha * acc_ref[g]
        e = e.astype(BF16)
        for i in range(pg):
            acc = acc + _dot_nt(e[:, i * PAGE_SIZE:(i + 1) * PAGE_SIZE], vt[i])
        acc_ref[g] = acc
        m_ref[g] = m_new

    @pl.when(j == pl.num_programs(1) - 1)
    def _():
        o_s = jnp.where(lane < HEAD_DIM, acc_ref[0] / l_ref[0], acc_ref[1] / l_ref[1])
        o_ref[...] = (part_ref[...] + gsel_ref[...] * o_s).astype(o_ref.dtype)


def _nsa_sample_b(cache_nsa_t, page_table, picks, qn, nkv_new, part, gsel, l, past, n_pick, pg):
    nb, n_pages = page_table.shape
    assert past % SEL_LEN == 0
    page_specs = [pl.BlockSpec((None, None, 2, NSA_KV_HEADS, HEAD_DIM, PAGE_SIZE),
                               lambda b, j, pt, pk, i=i: (l, pt[b, j * pg + i], 1, 0, 0, 0)) for i in range(pg)]
    row = lambda w: pl.BlockSpec((None, 1, w), lambda b, j, pt, pk: (b, 0, 0))
    tile = pl.BlockSpec((None, 8, LANES), lambda b, j, pt, pk: (b, 0, 0))
    return pl.pallas_call(
        functools.partial(_nsa_sample_b_body, past=past, n_pick=n_pick, pg=pg),
        out_shape=jax.ShapeDtypeStruct((nb, 8, LANES), BF16),
        grid_spec=pltpu.PrefetchScalarGridSpec(
            num_scalar_prefetch=2,
            grid=(nb, n_pages // pg),
            in_specs=page_specs + [row(NSA_HEADS * LANES), row(4 * NSA_KV), tile, tile],
            out_specs=tile,
            scratch_shapes=[pltpu.VMEM((NSA_KV_HEADS, 8, 1), F32), pltpu.VMEM((NSA_KV_HEADS, 8, 1), F32),
                            pltpu.VMEM((NSA_KV_HEADS, 8, LANES), F32)],
        ),
        compiler_params=_cparams(2),
        name="nsa_sample_selected",
    )(page_table, picks, *([cache_nsa_t] * pg), qn, nkv_new, part, gsel)


def _fox_sample_body(pt_ref, *refs, pg):
    pages = refs[:pg]
    lfs = refs[pg:2 * pg]
    q_ref, new_ref, sm_ref, ust_ref, aft_ref, o_ref, m_ref, l_ref, acc_ref, carry_ref = refs[2 * pg:]
    jj = pl.program_id(1)
    lane2 = lax.broadcasted_iota(jnp.int32, (8, FOX_W), 1)
    sub2 = lax.broadcasted_iota(jnp.int32, (8, FOX_W), 0)
    lane1 = lax.broadcasted_iota(jnp.int32, (8, LANES), 1)
    sub1 = lax.broadcasted_iota(jnp.int32, (8, LANES), 0)
    heads = [q_ref[:, h * LANES:(h + 1) * LANES].astype(F32) for h in range(FOX_HEADS)]
    zero = jnp.zeros((1, LANES), F32)
    even = jnp.where(lane1 < HEAD_DIM, _rows8([heads[h] if h % 2 == 0 else zero for h in range(FOX_HEADS)]), 0.0)
    odd = jnp.where(lane1 < HEAD_DIM, _rows8([heads[h] if h % 2 else zero for h in range(FOX_HEADS)]), 0.0)
    pair = even + pltpu.roll(odd, HEAD_DIM, axis=1)
    q32 = jnp.concatenate([jnp.where(sub1 // 2 == p, pair, 0.0) for p in range(FOX_HEADS // 2)], axis=1)

    def head_column(rows):
        return jnp.sum(jnp.transpose(rows), axis=-1, keepdims=True).reshape(FOX_HEADS, HEAD_DIM, 1)

    q3 = head_column(q32)

    @pl.when(jj == 0)
    def _():
        s_new = jnp.sum(q32 * new_ref[:, 0:FOX_W], axis=-1, keepdims=True)
        m_ref[...] = s_new[0:FOX_HEADS]
        l_ref[...] = jnp.ones(l_ref.shape, F32)
        v_rows = jnp.where(lane2 // HEAD_DIM == sub2, jnp.broadcast_to(new_ref[:, FOX_W:2 * FOX_W], (8, FOX_W)), 0.0)
        lane3 = lax.broadcasted_iota(jnp.int32, acc_ref.shape, 2)
        acc_ref[...] = jnp.where(lane3 == 0, head_column(v_rows), 0.0)
        carry_ref[...] = _row_to_col(sm_ref[...], SM_LOGF)

    lf = jnp.concatenate([r[...] for r in lfs], axis=0)
    ones = jnp.ones((LANES, LANES), BF16)
    within = jnp.zeros(lf.shape, F32)
    total = jnp.zeros(lf.shape, F32)
    for piece in _split3(lf):
        within = within + _dot(piece, ust_ref[...])
        total = total + _dot(piece, ones)
    after = jnp.zeros(lf.shape, F32)
    for piece in _split3(total):
        after = after + _dot(aft_ref[...], piece)
    carry = carry_ref[...]
    bias = within + after + jnp.concatenate([carry] * pg, axis=0)
    s3 = jnp.stack([jnp.sum(p[0] * q3, axis=1) for p in pages])
    s3 = s3 + bias.reshape(pg, 8, LANES)[:, 0:FOX_HEADS, :]
    m_old = m_ref[...]
    m_new = jnp.maximum(m_old, jnp.max(jnp.max(s3, axis=0), axis=-1, keepdims=True))
    e3 = jnp.exp(s3 - m_new[None])
    alpha = jnp.exp(m_old - m_new)
    l_ref[...] = alpha * l_ref[...] + jnp.sum(jnp.sum(e3, axis=0), axis=-1, keepdims=True)
    acc = alpha[:, :, None] * acc_ref[...]
    for i in range(pg):
        acc = acc + e3[i][:, None, :] * pages[i][1]
    acc_ref[...] = acc
    m_ref[...] = m_new
    carry_ref[...] = bias[0:8, 0:1] + lf[0:8, 0:1]

    @pl.when(jj == pl.num_programs(1) - 1)
    def _():
        o_ref[...] = (jnp.sum(acc_ref[...], axis=-1) / l_ref[...]).astype(o_ref.dtype)


def _fox_sample(cache_fox_t, lft, page_table, qf, fkv_new, sm, l, pg):
    nb, n_pages = page_table.shape
    steps = n_pages // pg
    ust = jnp.asarray(np.triu(np.ones((LANES, LANES), np.float32), 1).T, BF16)
    idx = np.arange(pg * 8)
    aft = ((idx[:, None] % 8 == idx[None, :] % 8) & (idx[None, :] // 8 > idx[:, None] // 8)).astype(np.float32)

    def page_of(b, j, pt, i):
        return pt[b, (steps - 1 - j) * pg + i]

    page_specs = [pl.BlockSpec((None, None, 2, FOX_HEADS, HEAD_DIM, PAGE_SIZE),
                               lambda b, j, pt, i=i: (l, page_of(b, j, pt, i), 0, 0, 0, 0)) for i in range(pg)]
    lf_specs = [pl.BlockSpec((None, None, 8, PAGE_SIZE),
                             lambda b, j, pt, i=i: (l, page_of(b, j, pt, i), 0, 0)) for i in range(pg)]
    row = lambda w: pl.BlockSpec((None, 1, w), lambda b, j, pt: (b, 0, 0))
    return pl.pallas_call(
        functools.partial(_fox_sample_body, pg=pg),
        out_shape=jax.ShapeDtypeStruct((nb, FOX_HEADS, HEAD_DIM), BF16),
        grid_spec=pltpu.PrefetchScalarGridSpec(
            num_scalar_prefetch=1,
            grid=(nb, steps),
            in_specs=page_specs + lf_specs + [
                row(FOX_HEADS * LANES), row(2 * FOX_W), row(LANES),
                _const_spec((LANES, LANES), lambda b, j, pt: (0, 0)),
                _const_spec((pg * 8, pg * 8), lambda b, j, pt: (0, 0)),
            ],
            out_specs=pl.BlockSpec((None, FOX_HEADS, HEAD_DIM), lambda b, j, pt: (b, 0, 0)),
            scratch_shapes=[pltpu.VMEM((FOX_HEADS, 1), F32), pltpu.VMEM((FOX_HEADS, 1), F32),
                            pltpu.VMEM((FOX_HEADS, HEAD_DIM, PAGE_SIZE), F32), pltpu.VMEM((8, 1), F32)],
        ),
        compiler_params=_cparams(2),
        name="fox_attention_sample",
    )(page_table, *([cache_fox_t] * pg), *([lft] * pg), qf, fkv_new, sm, ust, jnp.asarray(aft, BF16))


def _mixer_sample(x, norm_g, packed, conv_p, ffn_w, caches, page_table, l, tiles):
    w_proj, b_small, w_out_p, w_cmp, pe_cmp = packed
    conv_w, conv_b, ln_g, ln_b = conv_p
    cache_nsa_t, state_win, state_win_t, cache_fox_t, lft, state_conv, state_conv_t = caches
    nb = x.shape[1]
    past = page_table.shape[1] * PAGE_SIZE
    u, qn, nkv, _, _, win, _, _, qf, fkv, _, _, sm = _proj(x, norm_g, w_proj, b_small, l, nb)
    per_seq = lambda a: a.reshape(nb, 1, a.shape[-1])
    yconv = _conv_sample(state_conv_t, u[0], conv_w, conv_b, ln_g, ln_b, l)
    fs = _compress_sample(cache_nsa_t, page_table, w_cmp, pe_cmp, l, tiles['cmp_pg'])
    part, gsel, picks = _nsa_sample_a(per_seq(qn), per_seq(sm), fs, state_win_t, per_seq(win), l, past)
    n_pick = min(SEL_TOPK, -(-(past + 1) // SEL_LEN))
    picks2 = picks[:, :NSA_KV_HEADS, :n_pick].reshape(nb, NSA_KV_HEADS * n_pick)
    onsa = _nsa_sample_b(cache_nsa_t, page_table, picks2, per_seq(qn), per_seq(nkv), part, gsel, l, past, n_pick,
                         tiles['fox_pg'])
    onsa = jnp.transpose(onsa[:, :NSA_GROUP], (1, 0, 2))[None]
    ofox = _fox_sample(cache_fox_t, lft, page_table, per_seq(qf), per_seq(fkv), per_seq(sm), l, tiles['fox_pg'])
    mixed = _out_proj_ffn(x, yconv[None], onsa, ofox.reshape(1, nb, FOX_W), w_out_p, norm_g, *ffn_w, l, nb)
    wb = state_win.shape[2]
    win_state = jnp.concatenate([state_win[l][:, 1:], win[0].reshape(nb, 1, 2, NSA_KV_HEADS, HEAD_DIM)], axis=1)
    conv_state = jnp.concatenate([state_conv[l][:, 1:], u[0][:, None, :]], axis=1)
    states = (nkv.reshape(nb, 1, 4, NSA_KV_HEADS, HEAD_DIM),
              win_state,
              fkv.reshape(nb, 1, 2, FOX_HEADS, HEAD_DIM),
              sm[0][:, None, SM_LOGF:SM_LOGF + FOX_HEADS],
              conv_state)
    return mixed, states


def _tiles(T, n_pages):
    return dict(tm=min(512, T), tc=min(512, T), nsa_tq=256, nsa_tk=min(2048, T), fox_tq=min(256, T),
                fox_tk=min(2048, T), cum_tk=min(512, T), cmp_pg=min(32, n_pages), fox_pg=min(32, n_pages))


def kernel(x_prompt, x_sample, cache_nsa_kv, state_nsa_win, cache_fox_kv, cache_fox_logf, state_conv,
           page_table, norm_g, ffn_w_gate, ffn_w_up, ffn_w_down, w_in, w_out, conv_w, conv_b,
           conv_ln_g, conv_ln_b, cmp_pe, cmp_w, nsa_gate_b, fox_forget_b):
    depth = w_in.shape[0]
    B, T, D = x_prompt.shape
    nb, ts, _ = x_sample.shape
    n_phys = cache_nsa_kv.shape[1]
    n_pages = page_table.shape[1]
    wb = state_nsa_win.shape[2]
    assert ts == 1 and D == D_MODEL and T >= WINDOW + 128 and wb == min(WINDOW, n_pages * PAGE_SIZE)
    tiles = _tiles(T, n_pages)
    assert n_pages % tiles['cmp_pg'] == 0 and n_pages % tiles['fox_pg'] == 0
    assert PAGE_SIZE % SEL_LEN == 0 and T % tiles['tm'] == 0 and T % tiles['fox_tk'] == 0 and T % tiles['nsa_tk'] == 0

    packed = _pack_params(w_in, w_out, cmp_pe, cmp_w, nsa_gate_b, fox_forget_b)
    wg, wu, wd = (w.astype(BF16) for w in (ffn_w_gate, ffn_w_up, ffn_w_down))
    ng = norm_g[:, :, None, :]
    conv_p = (conv_w, conv_b[:, None, :], conv_ln_g[:, None, :], conv_ln_b[:, None, :])
    rows_last = (0, 1, 3, 4, 5, 2)
    cache_nsa_t = jnp.transpose(cache_nsa_kv, rows_last)
    cache_fox_t = jnp.transpose(cache_fox_kv, rows_last)
    state_win_t = jnp.transpose(state_nsa_win, rows_last)
    state_conv_t = jnp.transpose(state_conv, (0, 2, 1, 3))
    lft = jnp.pad(jnp.swapaxes(cache_fox_logf, 2, 3), ((0, 0), (0, 0), (0, 8 - FOX_HEADS), (0, 0)))
    caches = (cache_nsa_t, state_nsa_win, state_win_t, cache_fox_t, lft, state_conv, state_conv_t)

    xp = x_prompt
    xs = x_sample.reshape(1, nb, D)
    p_states, s_states = [], []
    for l in range(depth):
        xp = _ffn_first(xp, ng, wg, wu, wd, l, tiles['tm'])
        xs = _ffn_first(xs, ng, wg, wu, wd, l, nb)
        xp, ps = _mixer_prompt(xp, ng, packed, conv_p, (wg, wu, wd), l, tiles)
        xs, ss = _mixer_sample(xs, ng, packed, conv_p, (wg, wu, wd), caches, page_table, l, tiles)
        p_states.append(ps)
        s_states.append(ss)
    stack = lambda states, k: jnp.stack([s[k] for s in states])
    return (xp, xs.reshape(nb, 1, D),
            *(stack(p_states, k) for k in range(5)),
            *(stack(s_states, k) for k in range(5)))
```
